```python
import math
import jax
import jax.numpy as jnp
from jax import lax
import numpy as np

D_MODEL = 2048
BATCH = 4
SEQ = 4096
DEPTH = 4

CHUNK = 64
N_META = 16
ROPE_THETA = 500000.0
NORM_EPS = 1e-5
N_A_LAYERS = DEPTH // 2
N_B_LAYERS = DEPTH - N_A_LAYERS

A_HEADS = 8
A_HEAD_DIM = D_MODEL // (2 * A_HEADS)
A_ROT = A_HEAD_DIM // 4
A_QBLOCK = 128

B_HEAD_DIM = 64
B_Q_HEADS = D_MODEL // B_HEAD_DIM
B_KV_HEADS = 4
B_GROUP = B_Q_HEADS // B_KV_HEADS
B_ROT = B_HEAD_DIM // 4
WINDOW = 128
WINDOW_CHUNKS = WINDOW // CHUNK

kernel_name = "hybrid_diffattn_yoco_swa_sinks"


def rms_norm(x, g):
    xf = x.astype(jnp.float32)
    y = xf * lax.rsqrt(jnp.mean(xf * xf, axis=-1, keepdims=True) + NORM_EPS)
    return (y * g.astype(jnp.float32)).astype(x.dtype)


def chunk_ids(pos):
    return jnp.where(pos < N_META, 0, (pos - N_META) // CHUNK + 1)


def rope_tables(pos, rot):
    inv = ROPE_THETA ** (-jnp.arange(0, rot, 2, dtype=jnp.float32) / rot)
    ang = pos.astype(jnp.float32)[:, None] * inv[None, :]
    return jnp.cos(ang), jnp.sin(ang)


def partial_rope(x, cos, sin):
    half = cos.shape[-1]
    c = cos[None, :, None, :].astype(x.dtype)
    s = sin[None, :, None, :].astype(x.dtype)
    x1 = x[..., :half]
    x2 = x[..., half:2 * half]
    return jnp.concatenate([x1 * c - x2 * s, x2 * c + x1 * s, x[..., 2 * half:]], axis=-1)


def diff_attention(q, k, v, lam, cid):
    L = q.shape[1]
    scale = A_HEAD_DIM ** -0.5
    outs = []
    for s in range(0, L, A_QBLOCK):
        e = min(L, s + A_QBLOCK)
        kend = min(L, e + CHUNK)
        sc = jnp.einsum('bqhd,bkhd->bhqk', q[:, s:e], k[:, :kend]).astype(jnp.float32) * scale
        mask = cid[None, :kend] <= cid[s:e, None]
        sc = jnp.where(mask[None, None], sc, -jnp.inf)
        p = jax.nn.softmax(sc, axis=-1)
        p = p.reshape(p.shape[0], A_HEADS, 2, e - s, kend)
        w = p[:, :, 0] - lam * p[:, :, 1]
        outs.append(jnp.einsum('bhqk,bkhd->bqhd', w.astype(v.dtype), v[:, :kend]))
    return jnp.concatenate(outs, axis=1)


def diff_attn_layer(x, norm_g, w_in, w_out, lq1, lk1, lq2, lk2, subln_g, lambda_init, cos, sin, cid):
    B, L, _ = x.shape
    h = rms_norm(x, norm_g)
    q, k, v, gate = jnp.split(h @ w_in, 4, axis=-1)
    q = partial_rope(q.reshape(B, L, 2 * A_HEADS, A_HEAD_DIM), cos, sin)
    k = partial_rope(k.reshape(B, L, 2 * A_HEADS, A_HEAD_DIM), cos, sin)
    v = v.reshape(B, L, A_HEADS, 2 * A_HEAD_DIM)
    f32 = jnp.float32
    lam = (jnp.exp(jnp.sum(lq1.astype(f32) * lk1.astype(f32)))
           - jnp.exp(jnp.sum(lq2.astype(f32) * lk2.astype(f32))) + lambda_init)
    o = diff_attention(q, k, v, lam, cid)
    o = rms_norm(o, subln_g) * (1.0 - lambda_init)
    o = o.reshape(B, L, D_MODEL) * jax.nn.silu(gate)
    return x + o @ w_out


def shared_kv(x, kv_norm, w_kv, cos, sin):
    B, L, _ = x.shape
    k, v = jnp.split(rms_norm(x, kv_norm) @ w_kv, 2, axis=-1)
    k = partial_rope(k.reshape(B, L, B_KV_HEADS, B_HEAD_DIM), cos, sin)
    v = v.reshape(B, L, B_KV_HEADS, B_HEAD_DIM)
    return k, v


def sink_softmax(sc, sink):
    m = jnp.maximum(jnp.max(sc, axis=-1, keepdims=True), sink)
    e = jnp.exp(sc - m)
    return e / (jnp.sum(e, axis=-1, keepdims=True) + jnp.exp(sink - m))


def swa_sink_attention(q, k, v, sinks):
    B, L = q.shape[:2]
    S = L - N_META
    NC = S // CHUNK
    scale = B_HEAD_DIM ** -0.5
    sink = sinks.astype(jnp.float32).reshape(B_KV_HEADS, B_GROUP)
    qm = q[:, :N_META].reshape(B, N_META, B_KV_HEADS, B_GROUP, B_HEAD_DIM)
    km, vm = k[:, :N_META], v[:, :N_META]
    sm = jnp.einsum('bqkgd,bskd->bkgqs', qm, km).astype(jnp.float32) * scale
    pm = sink_softmax(sm, sink[None, :, :, None, None])
    om = jnp.einsum('bkgqs,bskd->bqkgd', pm.astype(v.dtype), vm).reshape(B, N_META, D_MODEL)
    qr = q[:, N_META:].reshape(B, NC, CHUNK, B_KV_HEADS, B_GROUP, B_HEAD_DIM)
    kr = k[:, N_META:].reshape(B, NC, CHUNK, B_KV_HEADS, B_HEAD_DIM)
    vr = v[:, N_META:].reshape(B, NC, CHUNK, B_KV_HEADS, B_HEAD_DIM)
    pad = ((0, 0), (WINDOW_CHUNKS, 0), (0, 0), (0, 0), (0, 0))
    kp, vp = jnp.pad(kr, pad), jnp.pad(vr, pad)
    kb = jnp.concatenate([kp[:, j:j + NC] for j in range(WINDOW_CHUNKS + 1)], axis=2)
    vb = jnp.concatenate([vp[:, j:j + NC] for j in range(WINDOW_CHUNKS + 1)], axis=2)
    meta_shape = (B, NC, N_META, B_KV_HEADS, B_HEAD_DIM)
    kb = jnp.concatenate([jnp.broadcast_to(km[:, None], meta_shape), kb], axis=2)
    vb = jnp.concatenate([jnp.broadcast_to(vm[:, None], meta_shape), vb], axis=2)
    band_chunk = jnp.arange(NC)[:, None] - WINDOW_CHUNKS + jnp.arange(WINDOW_CHUNKS + 1)[None, :]
    valid = jnp.concatenate([jnp.ones((NC, N_META), dtype=bool),
                             jnp.repeat(band_chunk >= 0, CHUNK, axis=1)], axis=1)
    sr = jnp.einsum('bnqkgd,bnskd->bnkgqs', qr, kb).astype(jnp.float32) * scale
    sr = jnp.where(valid[None, :, None, None, None, :], sr, -jnp.inf)
    pr = sink_softmax(sr, sink[None, None, :, :, None, None])
    orr = jnp.einsum('bnkgqs,bnskd->bnqkgd', pr.astype(v.dtype), vb).reshape(B, S, D_MODEL)
    return jnp.concatenate([om, orr], axis=1)


def swa_layer(x, norm_g, w_in, w_out, sinks, k_sh, v_sh, cos, sin):
    B, L, _ = x.shape
    q, gate = jnp.split(rms_norm(x, norm_g) @ w_in, 2, axis=-1)
    q = partial_rope(q.reshape(B, L, B_Q_HEADS, B_HEAD_DIM), cos, sin)
    o = swa_sink_attention(q, k_sh, v_sh, sinks)
    return x + (o * jax.nn.silu(gate)) @ w_out


def setup_inputs(seed: int = 0) -> dict:
    key = jax.random.key(seed)
    ks = jax.random.split(key, 17)
    D = D_MODEL

    def nrm(k, shape, s):
        return jax.random.normal(k, shape, jnp.float32) * s

    return {
        'x': nrm(ks[0], (BATCH, SEQ, D), 1.0),
        'meta_tokens': nrm(ks[1], (N_META, D), 1.0),
        'a_norm': 1.0 + nrm(ks[2], (N_A_LAYERS, D), 0.02),
        'a_w_in': nrm(ks[3], (N_A_LAYERS, D, 4 * D), D ** -0.5),
        'a_w_out': nrm(ks[4], (N_A_LAYERS, D, D), D ** -0.5),
        'a_lambda_q1': nrm(ks[5], (N_A_LAYERS, A_HEAD_DIM), 0.1),
        'a_lambda_k1': nrm(ks[6], (N_A_LAYERS, A_HEAD_DIM), 0.1),
        'a_lambda_q2': nrm(ks[7], (N_A_LAYERS, A_HEAD_DIM), 0.1),
        'a_lambda_k2': nrm(ks[8], (N_A_LAYERS, A_HEAD_DIM), 0.1),
        'a_subln': 1.0 + nrm(ks[9], (N_A_LAYERS, 2 * A_HEAD_DIM), 0.02),
        'kv_norm': 1.0 + nrm(ks[10], (D,), 0.02),
        'w_kv': nrm(ks[11], (D, 2 * B_KV_HEADS * B_HEAD_DIM), D ** -0.5),
        'b_norm': 1.0 + nrm(ks[12], (N_B_LAYERS, D), 0.02),
        'b_w_in': nrm(ks[13], (N_B_LAYERS, D, 2 * D), D ** -0.5),
        'b_w_out': nrm(ks[14], (N_B_LAYERS, D, D), D ** -0.5),
        'b_sinks': nrm(ks[15], (N_B_LAYERS, B_Q_HEADS), 0.5),
        'final_norm': 1.0 + nrm(ks[16], (D,), 0.02),
    }


def reference(x, meta_tokens, a_norm, a_w_in, a_w_out, a_lambda_q1, a_lambda_k1, a_lambda_q2,
              a_lambda_k2, a_subln, kv_norm, w_kv, b_norm, b_w_in, b_w_out, b_sinks, final_norm):
    B = x.shape[0]
    meta = jnp.broadcast_to(meta_tokens.astype(x.dtype)[None], (B, N_META, D_MODEL))
    h = jnp.concatenate([meta, x], axis=1)
    L = h.shape[1]
    pos = jnp.arange(L, dtype=jnp.int32)
    cid = chunk_ids(pos)
    cos_a, sin_a = rope_tables(pos, A_ROT)
    cos_b, sin_b = rope_tables(pos, B_ROT)
    k_sh = v_sh = None
    for i in range(DEPTH):
        if i < N_A_LAYERS:
            lambda_init = 0.8 - 0.6 * math.exp(-0.3 * i)
            h = diff_attn_layer(h, a_norm[i], a_w_in[i], a_w_out[i], a_lambda_q1[i], a_lambda_k1[i],
                                a_lambda_q2[i], a_lambda_k2[i], a_subln[i], lambda_init,
                                cos_a, sin_a, cid)
        else:
            if i == N_A_LAYERS:
                k_sh, v_sh = shared_kv(h, kv_norm, w_kv, cos_b, sin_b)
            j = i - N_A_LAYERS
            h = swa_layer(h, b_norm[j], b_w_in[j], b_w_out[j], b_sinks[j], k_sh, v_sh, cos_b, sin_b)
    return rms_norm(h, final_norm)[:, N_META:]
```

```python
import functools
import math

import jax
import jax.numpy as jnp
from jax import lax
from jax.experimental import pallas as pl
from jax.experimental.pallas import tpu as pltpu

D_MODEL = 2048
N_META = 16
META_ROWS = 64
CHUNK = 64
CHUNK_SHIFT = 6
ROPE_THETA = 500000.0
NORM_EPS = 1e-5
A_HEADS = 8
A_HEAD_DIM = 128
A_ROT_HALF = 16
B_HEAD_DIM = 64
B_KV_HEADS = 4
B_ROT_HALF = 8
LANES = 128
VMEM_LIMIT = 48 * 1024 * 1024

F32 = jnp.float32
BF16 = jnp.bfloat16
_NT = (((1,), (1,)), ((), ()))


def _params(*sem):
    return pltpu.CompilerParams(dimension_semantics=sem, vmem_limit_bytes=VMEM_LIMIT)


def _norm_kernel(x_ref, g_ref, o_ref):
    x = x_ref[...]
    r = lax.rsqrt(jnp.mean(x * x, axis=-1, keepdims=True) + NORM_EPS)
    o_ref[...] = (x * r * g_ref[...]).astype(o_ref.dtype)


def _rms_norm(x, g, tm):
    m = x.shape[0]
    return pl.pallas_call(
        _norm_kernel,
        grid=(m // tm,),
        in_specs=[pl.BlockSpec((tm, D_MODEL), lambda i: (i, 0)),
                  pl.BlockSpec((1, D_MODEL), lambda i: (0, 0))],
        out_specs=pl.BlockSpec((tm, D_MODEL), lambda i: (i, 0)),
        out_shape=jax.ShapeDtypeStruct((m, D_MODEL), BF16),
        compiler_params=_params("parallel"),
        name="rms_norm",
    )(x, g.reshape(1, D_MODEL))


def _proj_kernel(a_ref, w_ref, c_ref, s1_ref, s2_ref, o_ref, *, n_rope_tiles, n_q_tiles,
                 q_scale, half):
    j = pl.program_id(1)
    acc = jnp.dot(a_ref[...], w_ref[...], preferred_element_type=F32)
    tn = acc.shape[1]

    @pl.when(j < n_rope_tiles)
    def _():
        scale = jnp.where(j < n_q_tiles, q_scale, 1.0).astype(F32)
        c = c_ref[...] * scale
        s1 = s1_ref[...] * scale
        s2 = s2_ref[...] * scale
        for cb in range(tn // LANES):
            x = acc[:, cb * LANES:(cb + 1) * LANES]
            y = (x * c + pltpu.roll(x, LANES - half, 1) * s1 + pltpu.roll(x, half, 1) * s2)
            o_ref[:, cb * LANES:(cb + 1) * LANES] = y.astype(o_ref.dtype)

    @pl.when(j >= n_rope_tiles)
    def _():
        o_ref[...] = acc.astype(o_ref.dtype)


def _proj(a, w, tables, *, tm, tn, rope_cols, q_cols, q_scale, half):
    m, n = a.shape[0], w.shape[1]
    tab_blocks = tables[0].shape[0] // tm
    tab_spec = pl.BlockSpec((tm, LANES), lambda i, j: (i % tab_blocks, 0))
    kern = functools.partial(_proj_kernel, n_rope_tiles=rope_cols // tn, n_q_tiles=q_cols // tn,
                             q_scale=q_scale, half=half)
    return pl.pallas_call(
        kern,
        grid=(m // tm, n // tn),
        in_specs=[pl.BlockSpec((tm, D_MODEL), lambda i, j: (i, 0)),
                  pl.BlockSpec((D_MODEL, tn), lambda i, j: (0, j)),
                  tab_spec, tab_spec, tab_spec],
        out_specs=pl.BlockSpec((tm, tn), lambda i, j: (i, j)),
        out_shape=jax.ShapeDtypeStruct((m, n), BF16),
        compiler_params=_params("parallel", "arbitrary"),
        name="proj_rope",
    )(a, w, *tables)


def _rope_tables(pos, head_dim, half):
    inv = ROPE_THETA ** (-jnp.arange(0, 2 * half, 2, dtype=F32) / (2 * half))
    ang = pos.astype(F32)[:, None] * inv[None, :]
    cos, sin = jnp.cos(ang), jnp.sin(ang)
    p = pos.shape[0]
    ones = jnp.ones((p, head_dim - 2 * half), F32)
    zeros = jnp.zeros((p, head_dim - 2 * half), F32)
    zh = jnp.zeros((p, half), F32)
    c = jnp.concatenate([cos, cos, ones], axis=1)
    s1 = jnp.concatenate([-sin, zh, zeros], axis=1)
    s2 = jnp.concatenate([zh, sin, zeros], axis=1)
    rep = LANES // head_dim
    return tuple(jnp.tile(t, (1, rep)) for t in (c, s1, s2))


def _attn_a_kernel(lam_ref, q_ref, k_ref, v_ref, km_ref, vm_ref, gate_ref, g_ref, o_ref,
                   s_ref, acc_ref, *, tq, tk, lambda_init, real):
    i = pl.program_id(2)
    hd = A_HEAD_DIM
    q = q_ref[...]
    q1, q2 = q[:, :hd], q[:, hd:]
    km = km_ref[...]
    vm = vm_ref[...]
    neg_inf = jnp.float32(-jnp.inf)

    lv = lam_ref[...]
    lam = (jnp.exp(jnp.sum(lv[0:1] * lv[1:2], axis=-1, keepdims=True))
           - jnp.exp(jnp.sum(lv[2:3] * lv[3:4], axis=-1, keepdims=True)) + lambda_init)

    sm1 = lax.dot_general(q1, km[:, :hd], _NT, preferred_element_type=F32)
    sm2 = lax.dot_general(q2, km[:, hd:], _NT, preferred_element_type=F32)
    m1 = jnp.max(sm1, axis=-1, keepdims=True)
    m2 = jnp.max(sm2, axis=-1, keepdims=True)

    nblk = tk // LANES

    def lane_block_max(s):
        r = s[:, :LANES]
        for c in range(1, nblk):
            r = jnp.maximum(r, s[:, c * LANES:(c + 1) * LANES])
        return r

    if real:
        n_full = i // (tk // tq)

        def scores(j):
            kt = k_ref[pl.ds(pl.multiple_of(j * tk, tk), tk), :]
            return (lax.dot_general(q1, kt[:, :hd], _NT, preferred_element_type=F32),
                    lax.dot_general(q2, kt[:, hd:], _NT, preferred_element_type=F32))

        def body_a(j, carry):
            mr1, mr2 = carry
            s1, s2 = scores(j)
            s_ref[0, j] = s1
            s_ref[1, j] = s2
            return jnp.maximum(mr1, lane_block_max(s1)), jnp.maximum(mr2, lane_block_max(s2))

        init = jnp.full((tq, LANES), neg_inf, F32)
        mr1, mr2 = lax.fori_loop(0, n_full, body_a, (init, init))
        s1, s2 = scores(n_full)
        row = lax.broadcasted_iota(jnp.int32, (tq, tk), 0) + i * tq
        col = lax.broadcasted_iota(jnp.int32, (tq, tk), 1) + n_full * tk
        vis = (col >> CHUNK_SHIFT) <= (row >> CHUNK_SHIFT)
        s1 = jnp.where(vis, s1, neg_inf)
        s2 = jnp.where(vis, s2, neg_inf)
        s_ref[0, n_full] = s1
        s_ref[1, n_full] = s2
        mr1 = jnp.maximum(mr1, lane_block_max(s1))
        mr2 = jnp.maximum(mr2, lane_block_max(s2))
        m1 = jnp.maximum(m1, jnp.max(mr1, axis=-1, keepdims=True))
        m2 = jnp.maximum(m2, jnp.max(mr2, axis=-1, keepdims=True))

    em1 = jnp.exp(sm1 - m1)
    em2 = jnp.exp(sm2 - m2)
    l1 = jnp.sum(em1, axis=-1, keepdims=True)
    l2 = jnp.sum(em2, axis=-1, keepdims=True)
    pm = jnp.concatenate([em1, em2], axis=0).astype(BF16)
    acc_ref[...] = jnp.dot(pm, vm, preferred_element_type=F32)

    if real:
        m1b = jnp.broadcast_to(m1, (tq, LANES))
        m2b = jnp.broadcast_to(m2, (tq, LANES))

        def exp_blocks(h, j, mb, ls):
            blocks = []
            for c in range(nblk):
                e = jnp.exp(s_ref[h, j, :, c * LANES:(c + 1) * LANES] - mb)
                ls = ls + e
                blocks.append(e.astype(BF16))
            return jnp.concatenate(blocks, axis=1), ls

        def body_b(j, carry):
            ls1, ls2 = carry
            p1, ls1 = exp_blocks(0, j, m1b, ls1)
            p2, ls2 = exp_blocks(1, j, m2b, ls2)
            p = jnp.concatenate([p1, p2], axis=0)
            vt = v_ref[pl.ds(pl.multiple_of(j * tk, tk), tk), :]
            acc_ref[...] += jnp.dot(p, vt, preferred_element_type=F32)
            return ls1, ls2

        zero = jnp.zeros((tq, LANES), F32)
        ls1, ls2 = lax.fori_loop(0, n_full + 1, body_b, (zero, zero))
        l1 = l1 + jnp.sum(ls1, axis=-1, keepdims=True)
        l2 = l2 + jnp.sum(ls2, axis=-1, keepdims=True)

    o = acc_ref[:tq] * (1.0 / l1) - lam * (acc_ref[tq:] * (1.0 / l2))
    r = lax.rsqrt(jnp.mean(o * o, axis=-1, keepdims=True) + NORM_EPS)
    o = o * r * g_ref[...] * (1.0 - lambda_init)
    gate = gate_ref[...].astype(F32)
    o = o * (gate * jax.nn.sigmoid(gate))
    o_ref[...] = o.astype(o_ref.dtype)


def _attn_a(qkvg, qkvg_meta, lam_vecs, subln_g, *, batch, seq, tq, tk, lambda_init, real):
    nq = seq // tq
    hw = 2 * A_HEAD_DIM
    nh = A_HEADS
    kern = functools.partial(_attn_a_kernel, tq=tq, tk=tk, lambda_init=lambda_init, real=real)
    n_kt = max(seq // tk, 1)
    return pl.pallas_call(
        kern,
        grid=(batch, nh, nq),
        in_specs=[
            pl.BlockSpec((4, A_HEAD_DIM), lambda b, h, i: (0, 0)),
            pl.BlockSpec((tq, hw), lambda b, h, i: (b * nq + i, h)),
            pl.BlockSpec((seq, hw), lambda b, h, i: (b, nh + h)),
            pl.BlockSpec((seq, hw), lambda b, h, i: (b, 2 * nh + h)),
            pl.BlockSpec((N_META, hw), lambda b, h, i: (0, nh + h)),
            pl.BlockSpec((N_META, hw), lambda b, h, i: (0, 2 * nh + h)),
            pl.BlockSpec((tq, hw), lambda b, h, i: (b * nq + i, 3 * nh + h)),
            pl.BlockSpec((1, hw), lambda b, h, i: (0, 0)),
        ],
        out_specs=pl.BlockSpec((tq, hw), lambda b, h, i: (b * nq + i, h)),
        out_shape=jax.ShapeDtypeStruct((batch * seq, D_MODEL), BF16),
        scratch_shapes=[pltpu.VMEM((2, n_kt, tq, tk), F32),
                        pltpu.VMEM((2 * tq, hw), F32)],
        compiler_params=_params("parallel", "parallel", "arbitrary"),
        name="diff_attn" if real else "diff_attn_meta",
    )(lam_vecs, qkvg, qkvg, qkvg, qkvg_meta, qkvg_meta, qkvg, subln_g.reshape(1, hw))


def _swa_kernel(sink_ref, q_ref, kd_ref, vd_ref, kdm_ref, vdm_ref, gate_ref, o_ref):
    c = pl.program_id(1)
    tq = CHUNK
    win = 3 * CHUNK
    nk = 256
    start = pl.multiple_of(jnp.maximum(c * CHUNK - 2 * CHUNK, 0), CHUNK)
    n_vis = (jnp.minimum(c, 2) + 1) * CHUNK
    neg_inf = jnp.float32(-jnp.inf)

    colh = lax.broadcasted_iota(jnp.int32, (1, 2 * nk), 1) & (nk - 1)
    valid = jnp.logical_or(colh < N_META,
                           jnp.logical_and(colh >= CHUNK, colh < CHUNK + n_vis))
    bias = jnp.where(valid, 0.0, neg_inf).astype(F32)
    lane = lax.broadcasted_iota(jnp.int32, (nk, LANES), 1)
    low = lane < B_HEAD_DIM
    pad = jnp.zeros((CHUNK - N_META, LANES), BF16)
    zero_w = jnp.zeros((nk, LANES), BF16)

    def block_diag(meta_rows, win_rows):
        w = jnp.concatenate([meta_rows, pad, win_rows], axis=0)
        return jnp.concatenate([jnp.where(low, w, zero_w), jnp.where(low, zero_w, w)], axis=0)

    for kv in range(B_KV_HEADS):
        cs = slice(kv * LANES, (kv + 1) * LANES)
        kbd = block_diag(kdm_ref[:, cs], kd_ref[pl.ds(start, win), cs])
        vbd = block_diag(vdm_ref[:, cs], vd_ref[pl.ds(start, win), cs])
        q4 = jnp.concatenate(
            [q_ref[:, kv * 512 + p * LANES: kv * 512 + (p + 1) * LANES] for p in range(4)], axis=0)
        s = lax.dot_general(q4, kbd, _NT, preferred_element_type=F32) + bias
        p_rows = []
        for p in range(4):
            p_cols = []
            for hp in range(2):
                sk = sink_ref[kv * 8 + 2 * p + hp]
                sq = s[p * tq:(p + 1) * tq, hp * nk:(hp + 1) * nk]
                m = jnp.maximum(jnp.max(sq, axis=-1, keepdims=True), sk)
                e = jnp.exp(sq - m)
                den = jnp.sum(e, axis=-1, keepdims=True) + jnp.exp(sk - m)
                p_cols.append((e * (1.0 / den)).astype(BF16))
            p_rows.append(jnp.concatenate(p_cols, axis=1))
        pmat = jnp.concatenate(p_rows, axis=0)
        o4 = jnp.dot(pmat, vbd, preferred_element_type=F32)
        for p in range(4):
            osl = slice(kv * 512 + p * LANES, kv * 512 + (p + 1) * LANES)
            gate = gate_ref[:, osl].astype(F32)
            o_ref[:, osl] = (o4[p * tq:(p + 1) * tq] * (gate * jax.nn.sigmoid(gate))).astype(o_ref.dtype)


def _swa(qg, kvd, kvd_meta, sinks, *, batch, seq):
    nc = seq // CHUNK
    kw = B_KV_HEADS * LANES
    return pl.pallas_call(
        _swa_kernel,
        grid=(batch, nc),
        in_specs=[
            pl.BlockSpec(memory_space=pltpu.SMEM),
            pl.BlockSpec((CHUNK, D_MODEL), lambda b, c: (b * nc + c, 0)),
            pl.BlockSpec((seq, kw), lambda b, c: (b, 0)),
            pl.BlockSpec((seq, kw), lambda b, c: (b, 1)),
            pl.BlockSpec((N_META, kw), lambda b, c: (0, 0)),
            pl.BlockSpec((N_META, kw), lambda b, c: (0, 1)),
            pl.BlockSpec((CHUNK, D_MODEL), lambda b, c: (b * nc + c, 1)),
        ],
        out_specs=pl.BlockSpec((CHUNK, D_MODEL), lambda b, c: (b * nc + c, 0)),
        out_shape=jax.ShapeDtypeStruct((batch * seq, D_MODEL), BF16),
        compiler_params=_params("parallel", "arbitrary"),
        name="swa_attn",
    )(sinks, qg, kvd, kvd, kvd_meta, kvd_meta, qg)


def _out_kernel(*refs, n_norm, emit_h):
    o_ref, w_ref, h_ref = refs[:3]
    g_refs = refs[3:3 + n_norm]
    outs = refs[3 + n_norm:]
    acc = jnp.dot(o_ref[...], w_ref[...], preferred_element_type=F32) + h_ref[...]
    if emit_h:
        outs[0][...] = acc
        outs = outs[1:]
    r = lax.rsqrt(jnp.mean(acc * acc, axis=-1, keepdims=True) + NORM_EPS)
    xn = acc * r
    for g_ref, n_ref in zip(g_refs, outs):
        n_ref[...] = (xn * g_ref[...]).astype(n_ref.dtype)


def _out_proj(o, w, h, gains, *, tm, emit_h, norm_dtype):
    m = o.shape[0]
    row = pl.BlockSpec((tm, D_MODEL), lambda i: (i, 0))
    gspec = pl.BlockSpec((1, D_MODEL), lambda i: (0, 0))
    out_shape = ([jax.ShapeDtypeStruct((m, D_MODEL), F32)] if emit_h else []) + [
        jax.ShapeDtypeStruct((m, D_MODEL), norm_dtype) for _ in gains]
    kern = functools.partial(_out_kernel, n_norm=len(gains), emit_h=emit_h)
    return pl.pallas_call(
        kern,
        grid=(m // tm,),
        in_specs=[row, pl.BlockSpec((D_MODEL, D_MODEL), lambda i: (0, 0)), row] + [gspec] * len(gains),
        out_specs=[row] * len(out_shape),
        out_shape=out_shape,
        compiler_params=_params("parallel"),
        name="out_proj",
    )(o, w, h, *[g.reshape(1, D_MODEL) for g in gains])


def kernel(x, meta_tokens, a_norm, a_w_in, a_w_out, a_lambda_q1, a_lambda_k1, a_lambda_q2,
           a_lambda_k2, a_subln, kv_norm, w_kv, b_norm, b_w_in, b_w_out, b_sinks, final_norm):
    batch, seq, d = x.shape
    n_a = a_w_in.shape[0]
    n_b = b_w_in.shape[0]
    h_r = x.reshape(batch * seq, d)
    h_m = jnp.concatenate([meta_tokens.astype(x.dtype),
                           jnp.zeros((META_ROWS - N_META, d), x.dtype)], axis=0)

    pos_r = jnp.arange(seq, dtype=jnp.int32) + N_META
    pos_m = jnp.arange(META_ROWS, dtype=jnp.int32)
    tab_a_r = _rope_tables(pos_r, A_HEAD_DIM, A_ROT_HALF)
    tab_a_m = _rope_tables(pos_m, A_HEAD_DIM, A_ROT_HALF)
    tab_b_r = _rope_tables(pos_r, B_HEAD_DIM, B_ROT_HALF)
    tab_b_m = _rope_tables(pos_m, B_HEAD_DIM, B_ROT_HALF)

    a_w_in_b = a_w_in.astype(BF16)
    a_w_out_b = a_w_out.astype(BF16)
    b_w_in_b = b_w_in.astype(BF16)
    b_w_out_b = b_w_out.astype(BF16)
    wk, wv = jnp.split(w_kv.astype(BF16), 2, axis=1)

    def dup_heads(w):
        w = w.reshape(d, B_KV_HEADS, 1, B_HEAD_DIM)
        return jnp.broadcast_to(w, (d, B_KV_HEADS, 2, B_HEAD_DIM)).reshape(d, B_KV_HEADS * LANES)

    w_kvd = jnp.concatenate([dup_heads(wk), dup_heads(wv)], axis=1)

    tm_r = 1024
    a_scale = A_HEAD_DIM ** -0.5
    b_scale = B_HEAD_DIM ** -0.5

    hn_r = _rms_norm(h_r, a_norm[0], 512)
    hn_m = _rms_norm(h_m, a_norm[0], META_ROWS)
    for l in range(n_a):
        lambda_init = 0.8 - 0.6 * math.exp(-0.3 * l)
        proj_kw = dict(tn=512, rope_cols=2 * d, q_cols=d, q_scale=a_scale, half=A_ROT_HALF)
        qkvg_r = _proj(hn_r, a_w_in_b[l], tab_a_r, tm=tm_r, **proj_kw)
        qkvg_m = _proj(hn_m, a_w_in_b[l], tab_a_m, tm=META_ROWS, **proj_kw)
        lam_vecs = jnp.stack([a_lambda_q1[l], a_lambda_k1[l], a_lambda_q2[l], a_lambda_k2[l]]).astype(F32)
        o_r = _attn_a(qkvg_r, qkvg_m, lam_vecs, a_subln[l], batch=batch, seq=seq, tq=256, tk=512,
                      lambda_init=lambda_init, real=True)
        o_m = _attn_a(qkvg_m, qkvg_m, lam_vecs, a_subln[l], batch=1, seq=META_ROWS, tq=META_ROWS,
                      tk=META_ROWS, lambda_init=lambda_init, real=False)
        if l + 1 < n_a:
            gains_r = gains_m = [a_norm[l + 1]]
        else:
            gains_r, gains_m = [kv_norm, b_norm[0]], [kv_norm]
        outs_r = _out_proj(o_r, a_w_out_b[l], h_r, gains_r, tm=256, emit_h=True, norm_dtype=BF16)
        outs_m = _out_proj(o_m, a_w_out_b[l], h_m, gains_m, tm=META_ROWS, emit_h=True, norm_dtype=BF16)
        h_r, h_m = outs_r[0], outs_m[0]
        hn_r, hn_m = outs_r[1], outs_m[1]

    kv_kw = dict(tn=512, rope_cols=B_KV_HEADS * LANES, q_cols=0, q_scale=1.0, half=B_ROT_HALF)
    kvd_r = _proj(hn_r, w_kvd, tab_b_r, tm=tm_r, **kv_kw)
    kvd_m = _proj(hn_m, w_kvd, tab_b_m, tm=META_ROWS, **kv_kw)
    hn_r = outs_r[2]

    for l in range(n_b):
        qg = _proj(hn_r, b_w_in_b[l], tab_b_r, tm=tm_r, tn=512, rope_cols=d, q_cols=d,
                   q_scale=b_scale, half=B_ROT_HALF)
        o_r = _swa(qg, kvd_r, kvd_m, b_sinks[l].astype(F32), batch=batch, seq=seq)
        if l + 1 < n_b:
            h_r, hn_r = _out_proj(o_r, b_w_out_b[l], h_r, [b_norm[l + 1]], tm=256, emit_h=True,
                                  norm_dtype=BF16)
        else:
            (out,) = _out_proj(o_r, b_w_out_b[l], h_r, [final_norm], tm=256, emit_h=False,
                               norm_dtype=x.dtype)
    return out.reshape(batch, seq, d)
```

```python
import functools
import math

import jax
import jax.numpy as jnp
from jax import lax
from jax.experimental import pallas as pl
from jax.experimental.pallas import tpu as pltpu

D_MODEL = 2048
N_META = 16
META_ROWS = 64
CHUNK = 64
CHUNK_SHIFT = 6
ROPE_THETA = 500000.0
NORM_EPS = 1e-5
A_HEADS = 8
A_HEAD_DIM = 128
A_ROT_HALF = 16
B_HEAD_DIM = 64
B_KV_HEADS = 4
B_ROT_HALF = 8
LANES = 128
VMEM_LIMIT = 56 * 1024 * 1024
LOG2E = 1.4426950408889634

F32 = jnp.float32
BF16 = jnp.bfloat16
_NT = (((1,), (1,)), ((), ()))


def _params(*sem):
    return pltpu.CompilerParams(dimension_semantics=sem, vmem_limit_bytes=VMEM_LIMIT)


def _norm_kernel(x_ref, g_ref, o_ref):
    x = x_ref[...]
    r = lax.rsqrt(jnp.mean(x * x, axis=-1, keepdims=True) + NORM_EPS)
    o_ref[...] = (x * r * g_ref[...]).astype(o_ref.dtype)


def _rms_norm(x, g, tm):
    m = x.shape[0]
    return pl.pallas_call(
        _norm_kernel,
        grid=(m // tm,),
        in_specs=[pl.BlockSpec((tm, D_MODEL), lambda i: (i, 0)),
                  pl.BlockSpec((1, D_MODEL), lambda i: (0, 0))],
        out_specs=pl.BlockSpec((tm, D_MODEL), lambda i: (i, 0)),
        out_shape=jax.ShapeDtypeStruct((m, D_MODEL), BF16),
        compiler_params=_params("parallel"),
        name="rms_norm",
    )(x, g.reshape(1, D_MODEL))


def _proj_kernel(a_ref, w_ref, c_ref, s1_ref, s2_ref, o_ref, *, n_rope_tiles, n_q_tiles,
                 q_scale, half):
    j = pl.program_id(1)
    acc = jnp.dot(a_ref[...], w_ref[...], preferred_element_type=F32)
    tn = acc.shape[1]

    @pl.when(j < n_rope_tiles)
    def _():
        scale = jnp.where(j < n_q_tiles, q_scale, 1.0).astype(F32)
        c = c_ref[...] * scale
        s1 = s1_ref[...] * scale
        s2 = s2_ref[...] * scale
        for cb in range(tn // LANES):
            x = acc[:, cb * LANES:(cb + 1) * LANES]
            y = (x * c + pltpu.roll(x, LANES - half, 1) * s1 + pltpu.roll(x, half, 1) * s2)
            o_ref[:, cb * LANES:(cb + 1) * LANES] = y.astype(o_ref.dtype)

    @pl.when(j >= n_rope_tiles)
    def _():
        o_ref[...] = acc.astype(o_ref.dtype)


def _proj(a, w, tables, *, tm, tn, rope_cols, q_cols, q_scale, half):
    m, n = a.shape[0], w.shape[1]
    tab_blocks = tables[0].shape[0] // tm
    tab_spec = pl.BlockSpec((tm, LANES), lambda i, j: (i % tab_blocks, 0))
    kern = functools.partial(_proj_kernel, n_rope_tiles=rope_cols // tn, n_q_tiles=q_cols // tn,
                             q_scale=q_scale, half=half)
    return pl.pallas_call(
        kern,
        grid=(m // tm, n // tn),
        in_specs=[pl.BlockSpec((tm, D_MODEL), lambda i, j: (i, 0)),
                  pl.BlockSpec((D_MODEL, tn), lambda i, j: (0, j)),
                  tab_spec, tab_spec, tab_spec],
        out_specs=pl.BlockSpec((tm, tn), lambda i, j: (i, j)),
        out_shape=jax.ShapeDtypeStruct((m, n), BF16),
        compiler_params=_params("parallel", "arbitrary"),
        name="proj_rope",
    )(a, w, *tables)


def _rope_tables(pos, head_dim, half):
    inv = ROPE_THETA ** (-jnp.arange(0, 2 * half, 2, dtype=F32) / (2 * half))
    ang = pos.astype(F32)[:, None] * inv[None, :]
    cos, sin = jnp.cos(ang), jnp.sin(ang)
    p = pos.shape[0]
    ones = jnp.ones((p, head_dim - 2 * half), F32)
    zeros = jnp.zeros((p, head_dim - 2 * half), F32)
    zh = jnp.zeros((p, half), F32)
    c = jnp.concatenate([cos, cos, ones], axis=1)
    s1 = jnp.concatenate([-sin, zh, zeros], axis=1)
    s2 = jnp.concatenate([zh, sin, zeros], axis=1)
    rep = LANES // head_dim
    return tuple(jnp.tile(t, (1, rep)) for t in (c, s1, s2))


def _lambda_value(lam_ref, lambda_init):
    lv = lam_ref[...]
    return (jnp.exp(jnp.sum(lv[0:1] * lv[1:2], axis=-1, keepdims=True))
            - jnp.exp(jnp.sum(lv[2:3] * lv[3:4], axis=-1, keepdims=True)) + lambda_init)


def _diff_epilogue(acc, l1, l2, lam, g, gate, lambda_init):
    tq = acc.shape[0] // 2
    o = acc[:tq] * (1.0 / l1) - lam * (acc[tq:] * (1.0 / l2))
    r = lax.rsqrt(jnp.mean(o * o, axis=-1, keepdims=True) + NORM_EPS)
    o = o * r * g * (1.0 - lambda_init)
    gate = gate.astype(F32)
    return o * (gate * jax.nn.sigmoid(gate))


def _lane_block_reduce(x, op):
    r = x[:, :LANES]
    for c in range(1, x.shape[1] // LANES):
        r = op(r, x[:, c * LANES:(c + 1) * LANES])
    return r


def _attn_a_kernel(lam_ref, dmat_ref, qa_ref, qb_ref, k_ref, v_ref, km_ref, vm_ref, ga_ref, gb_ref,
                   g_ref, o_ref, q_sc, s_ref, mr_ref, mb_ref, ls_ref, acc_ref, *, tq, n_tiles,
                   lambda_init):
    i = pl.program_id(2)
    hd = A_HEAD_DIM
    half = n_tiles // 2
    n_steps = n_tiles + 1
    neg_inf = jnp.float32(-jnp.inf)
    q_sc[0] = qa_ref[...]
    q_sc[1] = qb_ref[...]
    km = km_ref[...]
    vm = vm_ref[...]
    lam = _lambda_value(lam_ref, lambda_init)

    def step_info(t):
        if t == 0:
            return 0, 0, i * (tq // CHUNK)
        if t == n_steps - 1:
            return 1, n_tiles - 1 - i, 0
        if t >= half:
            return 1, t - i - 1, None
        low = t <= i
        sel = jnp.where(low, 0, 1)
        kt = jnp.where(low, t, t - i - 1)
        thr = (jnp.where(low, i, n_tiles - 1 - i) - kt) * (tq // CHUNK)
        return sel, kt, thr

    init = jnp.full((tq, LANES), neg_inf, F32)
    for h in range(2):
        for sel in range(2):
            mr_ref[h, sel] = init
    for t in range(n_steps):
        sel, kt, thr = step_info(t)
        q = q_sc[sel]
        krows = k_ref[pl.ds(pl.multiple_of(kt * tq, tq), tq), :]
        for h in range(2):
            s = lax.dot_general(q[:, h * hd:(h + 1) * hd], krows[:, h * hd:(h + 1) * hd], _NT,
                                preferred_element_type=F32)
            if thr is not None:
                s = jnp.where(dmat_ref[...] <= thr, s, neg_inf)
            s_ref[h, t] = s
            mr_ref[h, sel] = jnp.maximum(mr_ref[h, sel], _lane_block_reduce(s, jnp.maximum))

    for sel in range(2):
        ems = []
        for h in range(2):
            sm = lax.dot_general(q_sc[sel][:, h * hd:(h + 1) * hd], km[:, h * hd:(h + 1) * hd], _NT,
                                 preferred_element_type=F32)
            m = jnp.maximum(jnp.max(sm, axis=-1, keepdims=True),
                            jnp.max(mr_ref[h, sel], axis=-1, keepdims=True))
            em = jnp.exp2(sm - m)
            mb_ref[h, sel] = jnp.broadcast_to(m, (tq, LANES))
            ls_ref[h, sel] = jnp.broadcast_to(jnp.sum(em, axis=-1, keepdims=True) * (1.0 / LANES),
                                              (tq, LANES))
            ems.append(em.astype(BF16))
        acc_ref[sel] = jnp.dot(jnp.concatenate(ems, axis=0), vm, preferred_element_type=F32)

    for t in range(n_steps):
        sel, kt, _ = step_info(t)
        ps = []
        for h in range(2):
            mb = mb_ref[h, sel]
            blocks = []
            ls = None
            for c in range(tq // LANES):
                e = jnp.exp2(s_ref[h, t, :, c * LANES:(c + 1) * LANES] - mb)
                ls = e if ls is None else ls + e
                blocks.append(e.astype(BF16))
            ls_ref[h, sel] = ls_ref[h, sel] + ls
            ps.append(jnp.concatenate(blocks, axis=1))
        p = jnp.concatenate(ps, axis=0)
        vrows = v_ref[pl.ds(pl.multiple_of(kt * tq, tq), tq), :]
        acc_ref[sel] = acc_ref[sel] + jnp.dot(p, vrows, preferred_element_type=F32)

    for sel, gate_ref in ((0, ga_ref), (1, gb_ref)):
        l1 = jnp.sum(ls_ref[0, sel], axis=-1, keepdims=True)
        l2 = jnp.sum(ls_ref[1, sel], axis=-1, keepdims=True)
        o = _diff_epilogue(acc_ref[sel], l1, l2, lam, g_ref[...], gate_ref[...], lambda_init)
        o_ref[sel] = o.astype(o_ref.dtype)


def _attn_a(qkvg, qkvg_meta, lam_vecs, subln_g, *, batch, seq, tq, lambda_init):
    n_tiles = seq // tq
    half = n_tiles // 2
    hw = 2 * A_HEAD_DIM
    nh = A_HEADS
    idx = jnp.arange(tq, dtype=jnp.int32) >> CHUNK_SHIFT
    dmat = idx[None, :] - idx[:, None]
    kern = functools.partial(_attn_a_kernel, tq=tq, n_tiles=n_tiles, lambda_init=lambda_init)
    lo = lambda b, h, i: b * n_tiles + i
    hi = lambda b, h, i: b * n_tiles + n_tiles - 1 - i
    return pl.pallas_call(
        kern,
        grid=(batch, nh, half),
        in_specs=[
            pl.BlockSpec((4, A_HEAD_DIM), lambda b, h, i: (0, 0)),
            pl.BlockSpec((tq, tq), lambda b, h, i: (0, 0)),
            pl.BlockSpec((tq, hw), lambda b, h, i: (lo(b, h, i), h)),
            pl.BlockSpec((tq, hw), lambda b, h, i: (hi(b, h, i), h)),
            pl.BlockSpec((seq, hw), lambda b, h, i: (b, nh + h)),
            pl.BlockSpec((seq, hw), lambda b, h, i: (b, 2 * nh + h)),
            pl.BlockSpec((N_META, hw), lambda b, h, i: (0, nh + h)),
            pl.BlockSpec((N_META, hw), lambda b, h, i: (0, 2 * nh + h)),
            pl.BlockSpec((tq, hw), lambda b, h, i: (lo(b, h, i), 3 * nh + h)),
            pl.BlockSpec((tq, hw), lambda b, h, i: (hi(b, h, i), 3 * nh + h)),
            pl.BlockSpec((1, hw), lambda b, h, i: (0, 0)),
        ],
        out_specs=pl.BlockSpec((2, None, None, tq, hw), lambda b, h, i: (0, b, i, 0, h)),
        out_shape=jax.ShapeDtypeStruct((2, batch, half, tq, D_MODEL), BF16),
        scratch_shapes=[pltpu.VMEM((2, tq, hw), BF16),
                        pltpu.VMEM((2, n_tiles + 1, tq, tq), F32),
                        pltpu.VMEM((2, 2, tq, LANES), F32),
                        pltpu.VMEM((2, 2, tq, LANES), F32),
                        pltpu.VMEM((2, 2, tq, LANES), F32),
                        pltpu.VMEM((2, 2 * tq, hw), F32)],
        compiler_params=_params("parallel", "parallel", "arbitrary"),
        name="diff_attn",
    )(lam_vecs, dmat, qkvg, qkvg, qkvg, qkvg, qkvg_meta, qkvg_meta, qkvg, qkvg,
      subln_g.reshape(1, hw))


def _attn_a_row_block(r, *, batch, seq, tq, tm):
    n_tiles = seq // tq
    half = n_tiles // 2
    sub = tq // tm
    b = r // (n_tiles * sub)
    u = r % (n_tiles * sub)
    tile, within = u // sub, u % sub
    hi = tile // half
    pos = jnp.where(hi == 0, tile, n_tiles - 1 - tile)
    return ((hi * batch + b) * half + pos) * sub + within


def _attn_a_meta_kernel(lam_ref, q_ref, km_ref, vm_ref, gate_ref, g_ref, o_ref, *, lambda_init):
    hd = A_HEAD_DIM
    q = q_ref[...]
    km = km_ref[...]
    ems, ls = [], []
    for h in range(2):
        sm = lax.dot_general(q[:, h * hd:(h + 1) * hd], km[:, h * hd:(h + 1) * hd], _NT,
                             preferred_element_type=F32)
        em = jnp.exp2(sm - jnp.max(sm, axis=-1, keepdims=True))
        ls.append(jnp.sum(em, axis=-1, keepdims=True))
        ems.append(em.astype(BF16))
    acc = jnp.dot(jnp.concatenate(ems, axis=0), vm_ref[...], preferred_element_type=F32)
    lam = _lambda_value(lam_ref, lambda_init)
    o = _diff_epilogue(acc, ls[0], ls[1], lam, g_ref[...], gate_ref[...], lambda_init)
    o_ref[...] = o.astype(o_ref.dtype)


def _attn_a_meta(qkvg_meta, lam_vecs, subln_g, *, lambda_init):
    hw = 2 * A_HEAD_DIM
    nh = A_HEADS
    kern = functools.partial(_attn_a_meta_kernel, lambda_init=lambda_init)
    return pl.pallas_call(
        kern,
        grid=(nh,),
        in_specs=[
            pl.BlockSpec((4, A_HEAD_DIM), lambda h: (0, 0)),
            pl.BlockSpec((META_ROWS, hw), lambda h: (0, h)),
            pl.BlockSpec((N_META, hw), lambda h: (0, nh + h)),
            pl.BlockSpec((N_META, hw), lambda h: (0, 2 * nh + h)),
            pl.BlockSpec((META_ROWS, hw), lambda h: (0, 3 * nh + h)),
            pl.BlockSpec((1, hw), lambda h: (0, 0)),
        ],
        out_specs=pl.BlockSpec((META_ROWS, hw), lambda h: (0, h)),
        out_shape=jax.ShapeDtypeStruct((META_ROWS, D_MODEL), BF16),
        compiler_params=_params("parallel"),
        name="diff_attn_meta",
    )(lam_vecs, qkvg_meta, qkvg_meta, qkvg_meta, qkvg_meta, subln_g.reshape(1, hw))


def _swa_kernel(sink_ref, q_ref, kd_ref, vd_ref, kdm_ref, vdm_ref, gate_ref, o_ref):
    c = pl.program_id(1)
    tq = CHUNK
    win = 3 * CHUNK
    nk = 256
    start = pl.multiple_of(jnp.maximum(c * CHUNK - 2 * CHUNK, 0), CHUNK)
    n_vis = (jnp.minimum(c, 2) + 1) * CHUNK
    neg_inf = jnp.float32(-jnp.inf)

    colh = lax.broadcasted_iota(jnp.int32, (1, 2 * nk), 1) & (nk - 1)
    valid = jnp.logical_or(colh < N_META,
                           jnp.logical_and(colh >= CHUNK, colh < CHUNK + n_vis))
    bias = jnp.where(valid, 0.0, neg_inf).astype(F32)
    lane = lax.broadcasted_iota(jnp.int32, (nk, LANES), 1)
    low = lane < B_HEAD_DIM
    pad = jnp.zeros((CHUNK - N_META, LANES), BF16)
    zero_w = jnp.zeros((nk, LANES), BF16)

    def block_diag(meta_rows, win_rows):
        w = jnp.concatenate([meta_rows, pad, win_rows], axis=0)
        return jnp.concatenate([jnp.where(low, w, zero_w), jnp.where(low, zero_w, w)], axis=0)

    for kv in range(B_KV_HEADS):
        cs = slice(kv * LANES, (kv + 1) * LANES)
        kbd = block_diag(kdm_ref[:, cs], kd_ref[pl.ds(start, win), cs])
        vbd = block_diag(vdm_ref[:, cs], vd_ref[pl.ds(start, win), cs])
        q4 = jnp.concatenate(
            [q_ref[:, kv * 512 + p * LANES: kv * 512 + (p + 1) * LANES] for p in range(4)], axis=0)
        s = lax.dot_general(q4, kbd, _NT, preferred_element_type=F32) + bias
        p_rows = []
        for p in range(4):
            p_cols = []
            for hp in range(2):
                sk = sink_ref[kv * 8 + 2 * p + hp]
                sq = s[p * tq:(p + 1) * tq, hp * nk:(hp + 1) * nk]
                m = jnp.maximum(jnp.max(sq, axis=-1, keepdims=True), sk)
                e = jnp.exp(sq - m)
                den = jnp.sum(e, axis=-1, keepdims=True) + jnp.exp(sk - m)
                p_cols.append((e * (1.0 / den)).astype(BF16))
            p_rows.append(jnp.concatenate(p_cols, axis=1))
        pmat = jnp.concatenate(p_rows, axis=0)
        o4 = jnp.dot(pmat, vbd, preferred_element_type=F32)
        for p in range(4):
            osl = slice(kv * 512 + p * LANES, kv * 512 + (p + 1) * LANES)
            gate = gate_ref[:, osl].astype(F32)
            o_ref[:, osl] = (o4[p * tq:(p + 1) * tq] * (gate * jax.nn.sigmoid(gate))).astype(o_ref.dtype)


def _swa(qg, kvd, kvd_meta, sinks, *, batch, seq):
    nc = seq // CHUNK
    kw = B_KV_HEADS * LANES
    return pl.pallas_call(
        _swa_kernel,
        grid=(batch, nc),
        in_specs=[
            pl.BlockSpec(memory_space=pltpu.SMEM),
            pl.BlockSpec((CHUNK, D_MODEL), lambda b, c: (b * nc + c, 0)),
            pl.BlockSpec((seq, kw), lambda b, c: (b, 0)),
            pl.BlockSpec((seq, kw), lambda b, c: (b, 1)),
            pl.BlockSpec((N_META, kw), lambda b, c: (0, 0)),
            pl.BlockSpec((N_META, kw), lambda b, c: (0, 1)),
            pl.BlockSpec((CHUNK, D_MODEL), lambda b, c: (b * nc + c, 1)),
        ],
        out_specs=pl.BlockSpec((CHUNK, D_MODEL), lambda b, c: (b * nc + c, 0)),
        out_shape=jax.ShapeDtypeStruct((batch * seq, D_MODEL), BF16),
        compiler_params=_params("parallel", "arbitrary"),
        name="swa_attn",
    )(sinks, qg, kvd, kvd, kvd_meta, kvd_meta, qg)


def _out_kernel(*refs, n_norm, emit_h):
    o_ref, w_ref, h_ref = refs[:3]
    g_refs = refs[3:3 + n_norm]
    outs = refs[3 + n_norm:]
    acc = jnp.dot(o_ref[...], w_ref[...], preferred_element_type=F32) + h_ref[...]
    if emit_h:
        outs[0][...] = acc
        outs = outs[1:]
    r = lax.rsqrt(jnp.mean(acc * acc, axis=-1, keepdims=True) + NORM_EPS)
    xn = acc * r
    for g_ref, n_ref in zip(g_refs, outs):
        n_ref[...] = (xn * g_ref[...]).astype(n_ref.dtype)


def _out_proj(o, w, h, gains, *, tm, emit_h, norm_dtype, o_row_block=None):
    m = h.shape[0]
    row = pl.BlockSpec((tm, D_MODEL), lambda i: (i, 0))
    o_spec = row if o_row_block is None else pl.BlockSpec((tm, D_MODEL), lambda i: (o_row_block(i), 0))
    gspec = pl.BlockSpec((1, D_MODEL), lambda i: (0, 0))
    out_shape = ([jax.ShapeDtypeStruct((m, D_MODEL), F32)] if emit_h else []) + [
        jax.ShapeDtypeStruct((m, D_MODEL), norm_dtype) for _ in gains]
    kern = functools.partial(_out_kernel, n_norm=len(gains), emit_h=emit_h)
    return pl.pallas_call(
        kern,
        grid=(m // tm,),
        in_specs=[o_spec, pl.BlockSpec((D_MODEL, D_MODEL), lambda i: (0, 0)), row] + [gspec] * len(gains),
        out_specs=[row] * len(out_shape),
        out_shape=out_shape,
        compiler_params=_params("parallel"),
        name="out_proj",
    )(o, w, h, *[g.reshape(1, D_MODEL) for g in gains])


def kernel(x, meta_tokens, a_norm, a_w_in, a_w_out, a_lambda_q1, a_lambda_k1, a_lambda_q2,
           a_lambda_k2, a_subln, kv_norm, w_kv, b_norm, b_w_in, b_w_out, b_sinks, final_norm):
    batch, seq, d = x.shape
    n_a = a_w_in.shape[0]
    n_b = b_w_in.shape[0]
    h_r = x.reshape(batch * seq, d)
    h_m = jnp.concatenate([meta_tokens.astype(x.dtype),
                           jnp.zeros((META_ROWS - N_META, d), x.dtype)], axis=0)

    pos_r = jnp.arange(seq, dtype=jnp.int32) + N_META
    pos_m = jnp.arange(META_ROWS, dtype=jnp.int32)
    tab_a_r = _rope_tables(pos_r, A_HEAD_DIM, A_ROT_HALF)
    tab_a_m = _rope_tables(pos_m, A_HEAD_DIM, A_ROT_HALF)
    tab_b_r = _rope_tables(pos_r, B_HEAD_DIM, B_ROT_HALF)
    tab_b_m = _rope_tables(pos_m, B_HEAD_DIM, B_ROT_HALF)

    a_w_in_b = a_w_in.astype(BF16)
    a_w_out_b = a_w_out.astype(BF16)
    b_w_in_b = b_w_in.astype(BF16)
    b_w_out_b = b_w_out.astype(BF16)
    wk, wv = jnp.split(w_kv.astype(BF16), 2, axis=1)

    def dup_heads(w):
        w = w.reshape(d, B_KV_HEADS, 1, B_HEAD_DIM)
        return jnp.broadcast_to(w, (d, B_KV_HEADS, 2, B_HEAD_DIM)).reshape(d, B_KV_HEADS * LANES)

    w_kvd = jnp.concatenate([dup_heads(wk), dup_heads(wv)], axis=1)

    tm_r = 1024
    tm_out = 256
    tq_a = 512
    a_scale = A_HEAD_DIM ** -0.5 * LOG2E
    b_scale = B_HEAD_DIM ** -0.5
    attn_rows = functools.partial(_attn_a_row_block, batch=batch, seq=seq, tq=tq_a, tm=tm_out)

    hn_r = _rms_norm(h_r, a_norm[0], 512)
    hn_m = _rms_norm(h_m, a_norm[0], META_ROWS)
    for l in range(n_a):
        lambda_init = 0.8 - 0.6 * math.exp(-0.3 * l)
        proj_kw = dict(tn=512, rope_cols=2 * d, q_cols=d, q_scale=a_scale, half=A_ROT_HALF)
        qkvg_r = _proj(hn_r, a_w_in_b[l], tab_a_r, tm=tm_r, **proj_kw)
        qkvg_m = _proj(hn_m, a_w_in_b[l], tab_a_m, tm=META_ROWS, **proj_kw)
        lam_vecs = jnp.stack([a_lambda_q1[l], a_lambda_k1[l], a_lambda_q2[l], a_lambda_k2[l]]).astype(F32)
        o_r = _attn_a(qkvg_r, qkvg_m, lam_vecs, a_subln[l], batch=batch, seq=seq, tq=tq_a,
                      lambda_init=lambda_init)
        o_m = _attn_a_meta(qkvg_m, lam_vecs, a_subln[l], lambda_init=lambda_init)
        if l + 1 < n_a:
            gains_r = gains_m = [a_norm[l + 1]]
        else:
            gains_r, gains_m = [kv_norm, b_norm[0]], [kv_norm]
        outs_r = _out_proj(o_r.reshape(batch * seq, d), a_w_out_b[l], h_r, gains_r, tm=tm_out,
                           emit_h=True, norm_dtype=BF16, o_row_block=attn_rows)
        outs_m = _out_proj(o_m, a_w_out_b[l], h_m, gains_m, tm=META_ROWS, emit_h=True, norm_dtype=BF16)
        h_r, h_m = outs_r[0], outs_m[0]
        hn_r, hn_m = outs_r[1], outs_m[1]

    kv_kw = dict(tn=512, rope_cols=B_KV_HEADS * LANES, q_cols=0, q_scale=1.0, half=B_ROT_HALF)
    kvd_r = _proj(hn_r, w_kvd, tab_b_r, tm=tm_r, **kv_kw)
    kvd_m = _proj(hn_m, w_kvd, tab_b_m, tm=META_ROWS, **kv_kw)
    hn_r = outs_r[2]

    for l in range(n_b):
        qg = _proj(hn_r, b_w_in_b[l], tab_b_r, tm=tm_r, tn=512, rope_cols=d, q_cols=d,
                   q_scale=b_scale, half=B_ROT_HALF)
        o_r = _swa(qg, kvd_r, kvd_m, b_sinks[l].astype(F32), batch=batch, seq=seq)
        if l + 1 < n_b:
            h_r, hn_r = _out_proj(o_r, b_w_out_b[l], h_r, [b_norm[l + 1]], tm=tm_out, emit_h=True,
                                  norm_dtype=BF16)
        else:
            (out,) = _out_proj(o_r, b_w_out_b[l], h_r, [final_norm], tm=tm_out, emit_h=False,
                               norm_dtype=x.dtype)
    return out.reshape(batch, seq, d)
```

```python
import functools
import math

import jax
import jax.numpy as jnp
from jax import lax
from jax.experimental import pallas as pl
from jax.experimental.pallas import tpu as pltpu

D_MODEL = 2048
N_META = 16
META_ROWS = 64
CHUNK = 64
CHUNK_SHIFT = 6
ROPE_THETA = 500000.0
NORM_EPS = 1e-5
A_HEADS = 8
A_HEAD_DIM = 128
A_ROT_HALF = 16
B_HEAD_DIM = 64
B_KV_HEADS = 4
B_ROT_HALF = 8
LANES = 128
VMEM_LIMIT = 56 * 1024 * 1024
LOG2E = 1.4426950408889634

F32 = jnp.float32
BF16 = jnp.bfloat16
_NT = (((1,), (1,)), ((), ()))


def _params(*sem):
    return pltpu.CompilerParams(dimension_semantics=sem, vmem_limit_bytes=VMEM_LIMIT)


def _norm_kernel(x_ref, g_ref, o_ref):
    x = x_ref[...]
    r = lax.rsqrt(jnp.mean(x * x, axis=-1, keepdims=True) + NORM_EPS)
    o_ref[...] = (x * r * g_ref[...]).astype(o_ref.dtype)


def _rms_norm(x, g, tm):
    m = x.shape[0]
    return pl.pallas_call(
        _norm_kernel,
        grid=(m // tm,),
        in_specs=[pl.BlockSpec((tm, D_MODEL), lambda i: (i, 0)),
                  pl.BlockSpec((1, D_MODEL), lambda i: (0, 0))],
        out_specs=pl.BlockSpec((tm, D_MODEL), lambda i: (i, 0)),
        out_shape=jax.ShapeDtypeStruct((m, D_MODEL), BF16),
        compiler_params=_params("parallel"),
        name="rms_norm",
    )(x, g.reshape(1, D_MODEL))


PROJ_SUB_ROWS = 256


def _proj_kernel(a_ref, w_ref, c_ref, s1_ref, s2_ref, o_ref, *, half):
    tm, tn = o_ref.shape
    sub = min(tm, PROJ_SUB_ROWS)
    w = w_ref[...]
    for rb in range(tm // sub):
        rows = slice(rb * sub, (rb + 1) * sub)
        acc = jnp.dot(a_ref[rows, :], w, preferred_element_type=F32)
        c, s1, s2 = c_ref[rows, :], s1_ref[rows, :], s2_ref[rows, :]
        for cb in range(tn // LANES):
            x = acc[:, cb * LANES:(cb + 1) * LANES]
            y = (x * c + pltpu.roll(x, LANES - half, 1) * s1 + pltpu.roll(x, half, 1) * s2)
            o_ref[rows, cb * LANES:(cb + 1) * LANES] = y.astype(o_ref.dtype)


def _proj(a, w, tables, *, tm, tn, rope_cols, q_cols, half):
    m, n = a.shape[0], w.shape[1]
    tab_blocks = (tables[0].shape[0] - tm) // (2 * tm)
    n_q_tiles, n_rope_tiles = q_cols // tn, rope_cols // tn

    def tab_index(i, j):
        pos = i % tab_blocks
        return (jnp.where(j < n_q_tiles, pos,
                          jnp.where(j < n_rope_tiles, tab_blocks + pos, 2 * tab_blocks)), 0)

    tab_spec = pl.BlockSpec((tm, LANES), tab_index)
    return pl.pallas_call(
        functools.partial(_proj_kernel, half=half),
        grid=(m // tm, n // tn),
        in_specs=[pl.BlockSpec((tm, D_MODEL), lambda i, j: (i, 0)),
                  pl.BlockSpec((D_MODEL, tn), lambda i, j: (0, j)),
                  tab_spec, tab_spec, tab_spec],
        out_specs=pl.BlockSpec((tm, tn), lambda i, j: (i, j)),
        out_shape=jax.ShapeDtypeStruct((m, n), BF16),
        compiler_params=_params("parallel", "arbitrary"),
        name="proj_rope",
    )(a, w, *tables)


def _proj_tables(pos, head_dim, half, q_scale, tm):
    ident = (jnp.ones((tm, LANES), F32), jnp.zeros((tm, LANES), F32), jnp.zeros((tm, LANES), F32))
    return tuple(jnp.concatenate([t * q_scale, t, e], axis=0)
                 for t, e in zip(_rope_tables(pos, head_dim, half), ident))


def _rope_tables(pos, head_dim, half):
    inv = ROPE_THETA ** (-jnp.arange(0, 2 * half, 2, dtype=F32) / (2 * half))
    ang = pos.astype(F32)[:, None] * inv[None, :]
    cos, sin = jnp.cos(ang), jnp.sin(ang)
    p = pos.shape[0]
    ones = jnp.ones((p, head_dim - 2 * half), F32)
    zeros = jnp.zeros((p, head_dim - 2 * half), F32)
    zh = jnp.zeros((p, half), F32)
    c = jnp.concatenate([cos, cos, ones], axis=1)
    s1 = jnp.concatenate([-sin, zh, zeros], axis=1)
    s2 = jnp.concatenate([zh, sin, zeros], axis=1)
    rep = LANES // head_dim
    return tuple(jnp.tile(t, (1, rep)) for t in (c, s1, s2))


def _lambda_value(lam_ref, lambda_init):
    lv = lam_ref[...]
    return (jnp.exp(jnp.sum(lv[0:1] * lv[1:2], axis=-1, keepdims=True))
            - jnp.exp(jnp.sum(lv[2:3] * lv[3:4], axis=-1, keepdims=True)) + lambda_init)


def _diff_epilogue(acc, l1, l2, lam, g, gate, lambda_init):
    tq = acc.shape[0] // 2
    o = acc[:tq] * (1.0 / l1) - lam * (acc[tq:] * (1.0 / l2))
    r = lax.rsqrt(jnp.mean(o * o, axis=-1, keepdims=True) + NORM_EPS)
    o = o * r * g * (1.0 - lambda_init)
    gate = gate.astype(F32)
    return o * (gate * jax.nn.sigmoid(gate))


def _lane_block_reduce(x, op):
    r = x[:, :LANES]
    for c in range(1, x.shape[1] // LANES):
        r = op(r, x[:, c * LANES:(c + 1) * LANES])
    return r


def _attn_a_kernel(lam_ref, dmat_ref, qa_ref, qb_ref, k_ref, v_ref, km_ref, vm_ref, ga_ref, gb_ref,
                   g_ref, o_ref, q_sc, s_ref, mr_ref, mb_ref, ls_ref, acc_ref, *, tq, n_tiles,
                   lambda_init):
    i = pl.program_id(2)
    hd = A_HEAD_DIM
    half = n_tiles // 2
    n_steps = n_tiles + 1
    neg_inf = jnp.float32(-jnp.inf)
    q_sc[0] = qa_ref[...]
    q_sc[1] = qb_ref[...]
    km = km_ref[...]
    vm = vm_ref[...]
    lam = _lambda_value(lam_ref, lambda_init)

    def step_info(t):
        if t == 0:
            return 0, 0, i * (tq // CHUNK)
        if t == n_steps - 1:
            return 1, n_tiles - 1 - i, 0
        if t >= half:
            return 1, t - i - 1, None
        low = t <= i
        sel = jnp.where(low, 0, 1)
        kt = jnp.where(low, t, t - i - 1)
        thr = (jnp.where(low, i, n_tiles - 1 - i) - kt) * (tq // CHUNK)
        return sel, kt, thr

    init = jnp.full((tq, LANES), neg_inf, F32)
    for h in range(2):
        for sel in range(2):
            mr_ref[h, sel] = init
    for t in range(n_steps):
        sel, kt, thr = step_info(t)
        q = q_sc[sel]
        krows = k_ref[pl.ds(pl.multiple_of(kt * tq, tq), tq), :]
        for h in range(2):
            s = lax.dot_general(q[:, h * hd:(h + 1) * hd], krows[:, h * hd:(h + 1) * hd], _NT,
                                preferred_element_type=F32)
            if thr is not None:
                s = jnp.where(dmat_ref[...] <= thr, s, neg_inf)
            s_ref[h, t] = s
            mr_ref[h, sel] = jnp.maximum(mr_ref[h, sel], _lane_block_reduce(s, jnp.maximum))

    for sel in range(2):
        ems = []
        for h in range(2):
            sm = lax.dot_general(q_sc[sel][:, h * hd:(h + 1) * hd], km[:, h * hd:(h + 1) * hd], _NT,
                                 preferred_element_type=F32)
            m = jnp.maximum(jnp.max(sm, axis=-1, keepdims=True),
                            jnp.max(mr_ref[h, sel], axis=-1, keepdims=True))
            em = jnp.exp2(sm - m)
            mb_ref[h, sel] = jnp.broadcast_to(m, (tq, LANES))
            ls_ref[h, sel] = jnp.broadcast_to(jnp.sum(em, axis=-1, keepdims=True) * (1.0 / LANES),
                                              (tq, LANES))
            ems.append(em.astype(BF16))
        acc_ref[sel] = jnp.dot(jnp.concatenate(ems, axis=0), vm, preferred_element_type=F32)

    for t in range(n_steps):
        sel, kt, _ = step_info(t)
        ps = []
        for h in range(2):
            mb = mb_ref[h, sel]
            blocks = []
            ls = None
            for c in range(tq // LANES):
                e = jnp.exp2(s_ref[h, t, :, c * LANES:(c + 1) * LANES] - mb)
                ls = e if ls is None else ls + e
                blocks.append(e.astype(BF16))
            ls_ref[h, sel] = ls_ref[h, sel] + ls
            ps.append(jnp.concatenate(blocks, axis=1))
        p = jnp.concatenate(ps, axis=0)
        vrows = v_ref[pl.ds(pl.multiple_of(kt * tq, tq), tq), :]
        acc_ref[sel] = acc_ref[sel] + jnp.dot(p, vrows, preferred_element_type=F32)

    for sel, gate_ref in ((0, ga_ref), (1, gb_ref)):
        l1 = jnp.sum(ls_ref[0, sel], axis=-1, keepdims=True)
        l2 = jnp.sum(ls_ref[1, sel], axis=-1, keepdims=True)
        o = _diff_epilogue(acc_ref[sel], l1, l2, lam, g_ref[...], gate_ref[...], lambda_init)
        o_ref[sel] = o.astype(o_ref.dtype)


def _attn_a(qkvg, qkvg_meta, lam_vecs, subln_g, *, batch, seq, tq, lambda_init):
    n_tiles = seq // tq
    half = n_tiles // 2
    hw = 2 * A_HEAD_DIM
    nh = A_HEADS
    idx = jnp.arange(tq, dtype=jnp.int32) >> CHUNK_SHIFT
    dmat = idx[None, :] - idx[:, None]
    kern = functools.partial(_attn_a_kernel, tq=tq, n_tiles=n_tiles, lambda_init=lambda_init)
    lo = lambda b, h, i: b * n_tiles + i
    hi = lambda b, h, i: b * n_tiles + n_tiles - 1 - i
    return pl.pallas_call(
        kern,
        grid=(batch, nh, half),
        in_specs=[
            pl.BlockSpec((4, A_HEAD_DIM), lambda b, h, i: (0, 0)),
            pl.BlockSpec((tq, tq), lambda b, h, i: (0, 0)),
            pl.BlockSpec((tq, hw), lambda b, h, i: (lo(b, h, i), h)),
            pl.BlockSpec((tq, hw), lambda b, h, i: (hi(b, h, i), h)),
            pl.BlockSpec((seq, hw), lambda b, h, i: (b, nh + h)),
            pl.BlockSpec((seq, hw), lambda b, h, i: (b, 2 * nh + h)),
            pl.BlockSpec((N_META, hw), lambda b, h, i: (0, nh + h)),
            pl.BlockSpec((N_META, hw), lambda b, h, i: (0, 2 * nh + h)),
            pl.BlockSpec((tq, hw), lambda b, h, i: (lo(b, h, i), 3 * nh + h)),
            pl.BlockSpec((tq, hw), lambda b, h, i: (hi(b, h, i), 3 * nh + h)),
            pl.BlockSpec((1, hw), lambda b, h, i: (0, 0)),
        ],
        out_specs=pl.BlockSpec((2, None, None, tq, hw), lambda b, h, i: (0, b, i, 0, h)),
        out_shape=jax.ShapeDtypeStruct((2, batch, half, tq, D_MODEL), BF16),
        scratch_shapes=[pltpu.VMEM((2, tq, hw), BF16),
                        pltpu.VMEM((2, n_tiles + 1, tq, tq), F32),
                        pltpu.VMEM((2, 2, tq, LANES), F32),
                        pltpu.VMEM((2, 2, tq, LANES), F32),
                        pltpu.VMEM((2, 2, tq, LANES), F32),
                        pltpu.VMEM((2, 2 * tq, hw), F32)],
        compiler_params=_params("parallel", "parallel", "arbitrary"),
        name="diff_attn",
    )(lam_vecs, dmat, qkvg, qkvg, qkvg, qkvg, qkvg_meta, qkvg_meta, qkvg, qkvg,
      subln_g.reshape(1, hw))


def _attn_a_row_block(r, *, batch, seq, tq, tm):
    n_tiles = seq // tq
    half = n_tiles // 2
    sub = tq // tm
    b = r // (n_tiles * sub)
    u = r % (n_tiles * sub)
    tile, within = u // sub, u % sub
    hi = tile // half
    pos = jnp.where(hi == 0, tile, n_tiles - 1 - tile)
    return ((hi * batch + b) * half + pos) * sub + within


def _attn_a_meta_kernel(lam_ref, q_ref, km_ref, vm_ref, gate_ref, g_ref, o_ref, *, lambda_init):
    hd = A_HEAD_DIM
    q = q_ref[...]
    km = km_ref[...]
    ems, ls = [], []
    for h in range(2):
        sm = lax.dot_general(q[:, h * hd:(h + 1) * hd], km[:, h * hd:(h + 1) * hd], _NT,
                             preferred_element_type=F32)
        em = jnp.exp2(sm - jnp.max(sm, axis=-1, keepdims=True))
        ls.append(jnp.sum(em, axis=-1, keepdims=True))
        ems.append(em.astype(BF16))
    acc = jnp.dot(jnp.concatenate(ems, axis=0), vm_ref[...], preferred_element_type=F32)
    lam = _lambda_value(lam_ref, lambda_init)
    o = _diff_epilogue(acc, ls[0], ls[1], lam, g_ref[...], gate_ref[...], lambda_init)
    o_ref[...] = o.astype(o_ref.dtype)


def _attn_a_meta(qkvg_meta, lam_vecs, subln_g, *, lambda_init):
    hw = 2 * A_HEAD_DIM
    nh = A_HEADS
    kern = functools.partial(_attn_a_meta_kernel, lambda_init=lambda_init)
    return pl.pallas_call(
        kern,
        grid=(nh,),
        in_specs=[
            pl.BlockSpec((4, A_HEAD_DIM), lambda h: (0, 0)),
            pl.BlockSpec((META_ROWS, hw), lambda h: (0, h)),
            pl.BlockSpec((N_META, hw), lambda h: (0, nh + h)),
            pl.BlockSpec((N_META, hw), lambda h: (0, 2 * nh + h)),
            pl.BlockSpec((META_ROWS, hw), lambda h: (0, 3 * nh + h)),
            pl.BlockSpec((1, hw), lambda h: (0, 0)),
        ],
        out_specs=pl.BlockSpec((META_ROWS, hw), lambda h: (0, h)),
        out_shape=jax.ShapeDtypeStruct((META_ROWS, D_MODEL), BF16),
        compiler_params=_params("parallel"),
        name="diff_attn_meta",
    )(lam_vecs, qkvg_meta, qkvg_meta, qkvg_meta, qkvg_meta, subln_g.reshape(1, hw))


SWA_KEYS = 256
SWA_SINK_COL = N_META


def _swa_kernel(sink_ref, q_ref, kd_ref, vd_ref, kdm_ref, vdm_ref, gate_ref, o_ref, *, n_chunks):
    step = pl.program_id(1)
    tq = CHUNK
    win = 3 * CHUNK
    nk = SWA_KEYS
    neg_inf = jnp.float32(-jnp.inf)

    colh = lax.broadcasted_iota(jnp.int32, (1, 2 * nk), 1) & (nk - 1)
    lane = lax.broadcasted_iota(jnp.int32, (nk, LANES), 1)
    lo_mask = jnp.where(lane < B_HEAD_DIM, 1.0, 0.0).astype(BF16)
    hi_mask = jnp.where(lane < B_HEAD_DIM, 0.0, 1.0).astype(BF16)
    pad = jnp.zeros((CHUNK - N_META, LANES), BF16)
    indicator = jnp.concatenate([lo_mask, hi_mask], axis=0)
    sink_rows = sink_ref[...] * LOG2E

    def block_diag(meta_rows, win_rows):
        w = jnp.concatenate([meta_rows, pad, win_rows], axis=0)
        return jnp.concatenate([w * lo_mask, w * hi_mask], axis=0)

    def chunk_body(cc, carry):
        c = step * n_chunks + cc
        r0 = pl.multiple_of(cc * CHUNK, CHUNK)
        start = pl.multiple_of(jnp.maximum(c * CHUNK - 2 * CHUNK, 0), CHUNK)
        n_vis = (jnp.minimum(c, 2) + 1) * CHUNK
        valid = jnp.logical_or(colh <= SWA_SINK_COL,
                               jnp.logical_and(colh >= CHUNK, colh < CHUNK + n_vis))
        for kv in range(B_KV_HEADS):
            cs = slice(kv * LANES, (kv + 1) * LANES)
            kbd = block_diag(kdm_ref[:, cs], kd_ref[pl.ds(start, win), cs])
            vbd = jnp.concatenate([block_diag(vdm_ref[:, cs], vd_ref[pl.ds(start, win), cs]),
                                   indicator], axis=1)
            q4 = jnp.concatenate(
                [q_ref[pl.ds(r0, tq), kv * 512 + p * LANES: kv * 512 + (p + 1) * LANES]
                 for p in range(4)], axis=0)
            s = lax.dot_general(q4, kbd, _NT, preferred_element_type=F32)
            p_rows = []
            for p in range(4):
                bias = jnp.where(valid, sink_rows[kv * 4 + p:kv * 4 + p + 1, :], neg_inf)
                sp = s[p * tq:(p + 1) * tq] + bias
                p_cols = []
                for hp in range(2):
                    sq = sp[:, hp * nk:(hp + 1) * nk]
                    e = jnp.exp2(sq - jnp.max(sq, axis=-1, keepdims=True))
                    p_cols.append(e.astype(BF16))
                p_rows.append(jnp.concatenate(p_cols, axis=1))
            pmat = jnp.concatenate(p_rows, axis=0)
            o4 = jnp.dot(pmat, vbd, preferred_element_type=F32)
            on = o4[:, :LANES] * (1.0 / o4[:, LANES:])
            for p in range(4):
                osl = slice(kv * 512 + p * LANES, kv * 512 + (p + 1) * LANES)
                gate = gate_ref[pl.ds(r0, tq), osl].astype(F32)
                o_ref[pl.ds(r0, tq), osl] = (on[p * tq:(p + 1) * tq]
                                             * (gate * jax.nn.sigmoid(gate))).astype(o_ref.dtype)
        return carry

    lax.fori_loop(0, n_chunks, chunk_body, 0, unroll=4)


def _swa(qg, kvd, kvd_meta, sinks, *, batch, seq, n_chunks):
    ns = seq // (CHUNK * n_chunks)
    rows = CHUNK * n_chunks
    kw = B_KV_HEADS * LANES
    sink_rows = jnp.zeros((B_KV_HEADS * 4, 2, SWA_KEYS), F32).at[:, :, SWA_SINK_COL].set(
        sinks.astype(F32).reshape(B_KV_HEADS * 4, 2)).reshape(B_KV_HEADS * 4, 2 * SWA_KEYS)
    return pl.pallas_call(
        functools.partial(_swa_kernel, n_chunks=n_chunks),
        grid=(batch, ns),
        in_specs=[
            pl.BlockSpec((B_KV_HEADS * 4, 2 * SWA_KEYS), lambda b, c: (0, 0)),
            pl.BlockSpec((rows, D_MODEL), lambda b, c: (b * ns + c, 0)),
            pl.BlockSpec((seq, kw), lambda b, c: (b, 0)),
            pl.BlockSpec((seq, kw), lambda b, c: (b, 1)),
            pl.BlockSpec((N_META, kw), lambda b, c: (0, 0)),
            pl.BlockSpec((N_META, kw), lambda b, c: (0, 1)),
            pl.BlockSpec((rows, D_MODEL), lambda b, c: (b * ns + c, 1)),
        ],
        out_specs=pl.BlockSpec((rows, D_MODEL), lambda b, c: (b * ns + c, 0)),
        out_shape=jax.ShapeDtypeStruct((batch * seq, D_MODEL), BF16),
        compiler_params=_params("parallel", "arbitrary"),
        name="swa_attn",
    )(sink_rows, qg, kvd, kvd, kvd_meta, kvd_meta, qg)


def _out_kernel(*refs, n_norm, emit_h):
    o_ref, w_ref, h_ref = refs[:3]
    g_refs = refs[3:3 + n_norm]
    outs = refs[3 + n_norm:]
    acc = jnp.dot(o_ref[...], w_ref[...], preferred_element_type=F32) + h_ref[...]
    if emit_h:
        outs[0][...] = acc
        outs = outs[1:]
    r = lax.rsqrt(jnp.mean(acc * acc, axis=-1, keepdims=True) + NORM_EPS)
    xn = acc * r
    for g_ref, n_ref in zip(g_refs, outs):
        n_ref[...] = (xn * g_ref[...]).astype(n_ref.dtype)


def _out_proj(o, w, h, gains, *, tm, emit_h, norm_dtype, o_row_block=None):
    m = h.shape[0]
    row = pl.BlockSpec((tm, D_MODEL), lambda i: (i, 0))
    o_spec = row if o_row_block is None else pl.BlockSpec((tm, D_MODEL), lambda i: (o_row_block(i), 0))
    gspec = pl.BlockSpec((1, D_MODEL), lambda i: (0, 0))
    out_shape = ([jax.ShapeDtypeStruct((m, D_MODEL), F32)] if emit_h else []) + [
        jax.ShapeDtypeStruct((m, D_MODEL), norm_dtype) for _ in gains]
    kern = functools.partial(_out_kernel, n_norm=len(gains), emit_h=emit_h)
    return pl.pallas_call(
        kern,
        grid=(m // tm,),
        in_specs=[o_spec, pl.BlockSpec((D_MODEL, D_MODEL), lambda i: (0, 0)), row] + [gspec] * len(gains),
        out_specs=[row] * len(out_shape),
        out_shape=out_shape,
        compiler_params=_params("parallel"),
        name="out_proj",
    )(o, w, h, *[g.reshape(1, D_MODEL) for g in gains])


def kernel(x, meta_tokens, a_norm, a_w_in, a_w_out, a_lambda_q1, a_lambda_k1, a_lambda_q2,
           a_lambda_k2, a_subln, kv_norm, w_kv, b_norm, b_w_in, b_w_out, b_sinks, final_norm):
    batch, seq, d = x.shape
    n_a = a_w_in.shape[0]
    n_b = b_w_in.shape[0]
    h_r = x.reshape(batch * seq, d)
    h_m = jnp.concatenate([meta_tokens.astype(x.dtype),
                           jnp.zeros((META_ROWS - N_META, d), x.dtype)], axis=0)

    tm_r = 1024
    tm_out = 256
    tq_a = 512
    a_scale = A_HEAD_DIM ** -0.5 * LOG2E
    b_scale = B_HEAD_DIM ** -0.5 * LOG2E
    attn_rows = functools.partial(_attn_a_row_block, batch=batch, seq=seq, tq=tq_a, tm=tm_out)

    pos_r = jnp.arange(seq, dtype=jnp.int32) + N_META
    pos_m = jnp.arange(META_ROWS, dtype=jnp.int32)
    tab_a_r = _proj_tables(pos_r, A_HEAD_DIM, A_ROT_HALF, a_scale, tm_r)
    tab_a_m = _proj_tables(pos_m, A_HEAD_DIM, A_ROT_HALF, a_scale, META_ROWS)
    tab_b_r = _proj_tables(pos_r, B_HEAD_DIM, B_ROT_HALF, b_scale, tm_r)
    tab_b_m = _proj_tables(pos_m, B_HEAD_DIM, B_ROT_HALF, b_scale, META_ROWS)

    wk, wv = jnp.split(w_kv.astype(BF16), 2, axis=1)

    def dup_heads(w):
        w = w.reshape(d, B_KV_HEADS, 1, B_HEAD_DIM)
        return jnp.broadcast_to(w, (d, B_KV_HEADS, 2, B_HEAD_DIM)).reshape(d, B_KV_HEADS * LANES)

    w_kvd = jnp.concatenate([dup_heads(wk), dup_heads(wv)], axis=1)

    hn_r = _rms_norm(h_r, a_norm[0], 512)
    hn_m = _rms_norm(h_m, a_norm[0], META_ROWS)
    for l in range(n_a):
        lambda_init = 0.8 - 0.6 * math.exp(-0.3 * l)
        proj_kw = dict(tn=1024, rope_cols=2 * d, q_cols=d, half=A_ROT_HALF)
        w_in = a_w_in[l].astype(BF16)
        w_out = a_w_out[l].astype(BF16)
        qkvg_r = _proj(hn_r, w_in, tab_a_r, tm=tm_r, **proj_kw)
        qkvg_m = _proj(hn_m, w_in, tab_a_m, tm=META_ROWS, **proj_kw)
        lam_vecs = jnp.stack([a_lambda_q1[l], a_lambda_k1[l], a_lambda_q2[l], a_lambda_k2[l]]).astype(F32)
        o_r = _attn_a(qkvg_r, qkvg_m, lam_vecs, a_subln[l], batch=batch, seq=seq, tq=tq_a,
                      lambda_init=lambda_init)
        o_m = _attn_a_meta(qkvg_m, lam_vecs, a_subln[l], lambda_init=lambda_init)
        if l + 1 < n_a:
            gains_r = gains_m = [a_norm[l + 1]]
        else:
            gains_r, gains_m = [kv_norm, b_norm[0]], [kv_norm]
        outs_r = _out_proj(o_r.reshape(batch * seq, d), w_out, h_r, gains_r, tm=tm_out,
                           emit_h=True, norm_dtype=BF16, o_row_block=attn_rows)
        outs_m = _out_proj(o_m, w_out, h_m, gains_m, tm=META_ROWS, emit_h=True, norm_dtype=BF16)
        h_r, h_m = outs_r[0], outs_m[0]
        hn_r, hn_m = outs_r[1], outs_m[1]

    kv_kw = dict(tn=512, rope_cols=B_KV_HEADS * LANES, q_cols=0, half=B_ROT_HALF)
    kvd_r = _proj(hn_r, w_kvd, tab_b_r, tm=tm_r, **kv_kw)
    kvd_m = _proj(hn_m, w_kvd, tab_b_m, tm=META_ROWS, **kv_kw)
    hn_r = outs_r[2]

    for l in range(n_b):
        qg = _proj(hn_r, b_w_in[l].astype(BF16), tab_b_r, tm=tm_r, tn=1024, rope_cols=d, q_cols=d,
                   half=B_ROT_HALF)
        o_r = _swa(qg, kvd_r, kvd_m, b_sinks[l], batch=batch, seq=seq, n_chunks=4)
        w_out = b_w_out[l].astype(BF16)
        if l + 1 < n_b:
            h_r, hn_r = _out_proj(o_r, w_out, h_r, [b_norm[l + 1]], tm=tm_out, emit_h=True,
                                  norm_dtype=BF16)
        else:
            (out,) = _out_proj(o_r, w_out, h_r, [final_norm], tm=tm_out, emit_h=False,
                               norm_dtype=x.dtype)
    return out.reshape(batch, seq, d)
```

```python
import functools
import math

import jax
import jax.numpy as jnp
from jax import lax
from jax.experimental import pallas as pl
from jax.experimental.pallas import tpu as pltpu

D_MODEL = 2048
N_META = 16
META_ROWS = 64
CHUNK = 64
CHUNK_SHIFT = 6
ROPE_THETA = 500000.0
NORM_EPS = 1e-5
A_HEADS = 8
A_HEAD_DIM = 128
A_ROT_HALF = 16
B_HEAD_DIM = 64
B_KV_HEADS = 4
B_ROT_HALF = 8
LANES = 128
VMEM_LIMIT = 56 * 1024 * 1024
LOG2E = 1.4426950408889634

F32 = jnp.float32
BF16 = jnp.bfloat16
_NT = (((1,), (1,)), ((), ()))


def _params(*sem):
    return pltpu.CompilerParams(dimension_semantics=sem, vmem_limit_bytes=VMEM_LIMIT)


def _norm_kernel(x_ref, g_ref, o_ref):
    x = x_ref[...]
    r = lax.rsqrt(jnp.mean(x * x, axis=-1, keepdims=True) + NORM_EPS)
    o_ref[...] = (x * r * g_ref[...]).astype(o_ref.dtype)


def _rms_norm(x, g, tm):
    m = x.shape[0]
    return pl.pallas_call(
        _norm_kernel,
        grid=(m // tm,),
        in_specs=[pl.BlockSpec((tm, D_MODEL), lambda i: (i, 0)),
                  pl.BlockSpec((1, D_MODEL), lambda i: (0, 0))],
        out_specs=pl.BlockSpec((tm, D_MODEL), lambda i: (i, 0)),
        out_shape=jax.ShapeDtypeStruct((m, D_MODEL), BF16),
        compiler_params=_params("parallel"),
        name="rms_norm",
    )(x, g.reshape(1, D_MODEL))


PROJ_SUB_ROWS = 256


def _proj_kernel(a_ref, w_ref, c_ref, s1_ref, s2_ref, o_ref, wb_ref, *, half):
    tm, tn = o_ref.shape
    sub = min(tm, PROJ_SUB_ROWS)

    @pl.when(pl.program_id(1) == 0)
    def _():
        wb_ref[...] = w_ref[...].astype(wb_ref.dtype)

    w = wb_ref[...]
    for rb in range(tm // sub):
        rows = slice(rb * sub, (rb + 1) * sub)
        acc = jnp.dot(a_ref[rows, :], w, preferred_element_type=F32)
        c, s1, s2 = c_ref[rows, :], s1_ref[rows, :], s2_ref[rows, :]
        for cb in range(tn // LANES):
            x = acc[:, cb * LANES:(cb + 1) * LANES]
            y = (x * c + pltpu.roll(x, LANES - half, 1) * s1 + pltpu.roll(x, half, 1) * s2)
            o_ref[rows, cb * LANES:(cb + 1) * LANES] = y.astype(o_ref.dtype)


def _proj(a, w, layer, tables, *, tm, tn, rope_cols, q_cols, half):
    m, n = a.shape[0], w.shape[2]
    tab_blocks = (tables[0].shape[0] - tm) // (2 * tm)
    n_q_tiles, n_rope_tiles = q_cols // tn, rope_cols // tn

    def tab_index(j, i):
        pos = i % tab_blocks
        return (jnp.where(j < n_q_tiles, pos,
                          jnp.where(j < n_rope_tiles, tab_blocks + pos, 2 * tab_blocks)), 0)

    tab_spec = pl.BlockSpec((tm, LANES), tab_index)
    return pl.pallas_call(
        functools.partial(_proj_kernel, half=half),
        grid=(n // tn, m // tm),
        in_specs=[pl.BlockSpec((tm, D_MODEL), lambda j, i: (i, 0)),
                  pl.BlockSpec((None, D_MODEL, tn), lambda j, i: (layer, 0, j)),
                  tab_spec, tab_spec, tab_spec],
        out_specs=pl.BlockSpec((tm, tn), lambda j, i: (i, j)),
        out_shape=jax.ShapeDtypeStruct((m, n), BF16),
        scratch_shapes=[pltpu.VMEM((D_MODEL, tn), BF16)],
        compiler_params=_params("arbitrary", "arbitrary"),
        name="proj_rope",
    )(a, w, *tables)


def _proj_tables(pos, head_dim, half, q_scale, tm):
    ident = (jnp.ones((tm, LANES), F32), jnp.zeros((tm, LANES), F32), jnp.zeros((tm, LANES), F32))
    return tuple(jnp.concatenate([t * q_scale, t, e], axis=0)
                 for t, e in zip(_rope_tables(pos, head_dim, half), ident))


def _rope_tables(pos, head_dim, half):
    inv = ROPE_THETA ** (-jnp.arange(0, 2 * half, 2, dtype=F32) / (2 * half))
    ang = pos.astype(F32)[:, None] * inv[None, :]
    cos, sin = jnp.cos(ang), jnp.sin(ang)
    p = pos.shape[0]
    ones = jnp.ones((p, head_dim - 2 * half), F32)
    zeros = jnp.zeros((p, head_dim - 2 * half), F32)
    zh = jnp.zeros((p, half), F32)
    c = jnp.concatenate([cos, cos, ones], axis=1)
    s1 = jnp.concatenate([-sin, zh, zeros], axis=1)
    s2 = jnp.concatenate([zh, sin, zeros], axis=1)
    rep = LANES // head_dim
    return tuple(jnp.tile(t, (1, rep)) for t in (c, s1, s2))


def _lambda_value(lam_ref, lambda_init):
    lv = lam_ref[...]
    return (jnp.exp(jnp.sum(lv[0:1] * lv[1:2], axis=-1, keepdims=True))
            - jnp.exp(jnp.sum(lv[2:3] * lv[3:4], axis=-1, keepdims=True)) + lambda_init)


def _diff_epilogue(acc, l1, l2, lam, g, gate, lambda_init):
    tq = acc.shape[0] // 2
    o = acc[:tq] * (1.0 / l1) - lam * (acc[tq:] * (1.0 / l2))
    r = lax.rsqrt(jnp.mean(o * o, axis=-1, keepdims=True) + NORM_EPS)
    o = o * r * g * (1.0 - lambda_init)
    gate = gate.astype(F32)
    return o * (gate * jax.nn.sigmoid(gate))


def _lane_block_reduce(x, op):
    r = x[:, :LANES]
    for c in range(1, x.shape[1] // LANES):
        r = op(r, x[:, c * LANES:(c + 1) * LANES])
    return r


def _attn_a_kernel(lam_ref, dmat_ref, qa_ref, qb_ref, k_ref, v_ref, km_ref, vm_ref, ga_ref, gb_ref,
                   g_ref, o_ref, q_sc, s_ref, mr_ref, mb_ref, ls_ref, acc_ref, *, tq, n_tiles,
                   lambda_init):
    i = pl.program_id(2)
    hd = A_HEAD_DIM
    half = n_tiles // 2
    n_steps = n_tiles + 1
    neg_inf = jnp.float32(-jnp.inf)
    q_sc[0] = qa_ref[...]
    q_sc[1] = qb_ref[...]
    km = km_ref[...]
    vm = vm_ref[...]
    lam = _lambda_value(lam_ref, lambda_init)

    def step_info(t):
        if t == 0:
            return 0, 0, i * (tq // CHUNK)
        if t == n_steps - 1:
            return 1, n_tiles - 1 - i, 0
        if t >= half:
            return 1, t - i - 1, None
        low = t <= i
        sel = jnp.where(low, 0, 1)
        kt = jnp.where(low, t, t - i - 1)
        thr = (jnp.where(low, i, n_tiles - 1 - i) - kt) * (tq // CHUNK)
        return sel, kt, thr

    init = jnp.full((tq, LANES), neg_inf, F32)
    for h in range(2):
        for sel in range(2):
            mr_ref[h, sel] = init
            ls_ref[h, sel] = jnp.zeros((tq, LANES), F32)
    acc_ref[...] = jnp.zeros(acc_ref.shape, F32)
    for t in range(n_steps):
        sel, kt, thr = step_info(t)
        q = q_sc[sel]
        krows = k_ref[pl.ds(pl.multiple_of(kt * tq, tq), tq), :]
        for h in range(2):
            s = lax.dot_general(q[:, h * hd:(h + 1) * hd], krows[:, h * hd:(h + 1) * hd], _NT,
                                preferred_element_type=F32)
            if thr is not None:
                s = jnp.where(dmat_ref[...] <= thr, s, neg_inf)
            s_ref[h, t] = s
            mr_ref[h, sel] = jnp.maximum(mr_ref[h, sel], _lane_block_reduce(s, jnp.maximum))

    def meta_scores(sel, h):
        return lax.dot_general(q_sc[sel][:, h * hd:(h + 1) * hd], km[:, h * hd:(h + 1) * hd], _NT,
                               preferred_element_type=F32)

    for sel in range(2):
        for h in range(2):
            m = jnp.maximum(jnp.max(meta_scores(sel, h), axis=-1, keepdims=True),
                            jnp.max(mr_ref[h, sel], axis=-1, keepdims=True))
            mb_ref[h, sel] = jnp.broadcast_to(m, (tq, LANES))

    for t in range(n_steps):
        sel, kt, _ = step_info(t)
        ps = []
        for h in range(2):
            mb = mb_ref[h, sel]
            blocks = []
            ls = None
            for c in range(tq // LANES):
                e = jnp.exp2(s_ref[h, t, :, c * LANES:(c + 1) * LANES] - mb)
                ls = e if ls is None else ls + e
                blocks.append(e.astype(BF16))
            ls_ref[h, sel] = ls_ref[h, sel] + ls
            ps.append(jnp.concatenate(blocks, axis=1))
        p = jnp.concatenate(ps, axis=0)
        vrows = v_ref[pl.ds(pl.multiple_of(kt * tq, tq), tq), :]
        acc_ref[sel] = acc_ref[sel] + jnp.dot(p, vrows, preferred_element_type=F32)

    for sel, gate_ref in ((0, ga_ref), (1, gb_ref)):
        ems, ls = [], []
        for h in range(2):
            em = jnp.exp2(meta_scores(sel, h) - mb_ref[h, sel, :, :1])
            ls.append(jnp.sum(ls_ref[h, sel], axis=-1, keepdims=True)
                      + jnp.sum(em, axis=-1, keepdims=True))
            ems.append(em.astype(BF16))
        acc = acc_ref[sel] + jnp.dot(jnp.concatenate(ems, axis=0), vm, preferred_element_type=F32)
        o = _diff_epilogue(acc, ls[0], ls[1], lam, g_ref[...], gate_ref[...], lambda_init)
        o_ref[sel] = o.astype(o_ref.dtype)


def _attn_a(qkvg, qkvg_meta, lam_vecs, subln_g, *, batch, seq, tq, lambda_init):
    n_tiles = seq // tq
    half = n_tiles // 2
    hw = 2 * A_HEAD_DIM
    nh = A_HEADS
    idx = jnp.arange(tq, dtype=jnp.int32) >> CHUNK_SHIFT
    dmat = idx[None, :] - idx[:, None]
    kern = functools.partial(_attn_a_kernel, tq=tq, n_tiles=n_tiles, lambda_init=lambda_init)
    lo = lambda b, h, i: b * n_tiles + i
    hi = lambda b, h, i: b * n_tiles + n_tiles - 1 - i
    return pl.pallas_call(
        kern,
        grid=(batch, nh, half),
        in_specs=[
            pl.BlockSpec((4, A_HEAD_DIM), lambda b, h, i: (0, 0)),
            pl.BlockSpec((tq, tq), lambda b, h, i: (0, 0)),
            pl.BlockSpec((tq, hw), lambda b, h, i: (lo(b, h, i), h)),
            pl.BlockSpec((tq, hw), lambda b, h, i: (hi(b, h, i), h)),
            pl.BlockSpec((seq, hw), lambda b, h, i: (b, nh + h)),
            pl.BlockSpec((seq, hw), lambda b, h, i: (b, 2 * nh + h)),
            pl.BlockSpec((N_META, hw), lambda b, h, i: (0, nh + h)),
            pl.BlockSpec((N_META, hw), lambda b, h, i: (0, 2 * nh + h)),
            pl.BlockSpec((tq, hw), lambda b, h, i: (lo(b, h, i), 3 * nh + h)),
            pl.BlockSpec((tq, hw), lambda b, h, i: (hi(b, h, i), 3 * nh + h)),
            pl.BlockSpec((1, hw), lambda b, h, i: (0, 0)),
        ],
        out_specs=pl.BlockSpec((2, None, None, tq, hw), lambda b, h, i: (0, b, i, 0, h)),
        out_shape=jax.ShapeDtypeStruct((2, batch, half, tq, D_MODEL), BF16),
        scratch_shapes=[pltpu.VMEM((2, tq, hw), BF16),
                        pltpu.VMEM((2, n_tiles + 1, tq, tq), F32),
                        pltpu.VMEM((2, 2, tq, LANES), F32),
                        pltpu.VMEM((2, 2, tq, LANES), F32),
                        pltpu.VMEM((2, 2, tq, LANES), F32),
                        pltpu.VMEM((2, 2 * tq, hw), F32)],
        compiler_params=_params("parallel", "parallel", "arbitrary"),
        name="diff_attn",
    )(lam_vecs, dmat, qkvg, qkvg, qkvg, qkvg, qkvg_meta, qkvg_meta, qkvg, qkvg,
      subln_g.reshape(1, hw))


def _attn_a_row_block(r, *, batch, seq, tq, tm):
    n_tiles = seq // tq
    half = n_tiles // 2
    sub = tq // tm
    b = r // (n_tiles * sub)
    u = r % (n_tiles * sub)
    tile, within = u // sub, u % sub
    hi = tile // half
    pos = jnp.where(hi == 0, tile, n_tiles - 1 - tile)
    return ((hi * batch + b) * half + pos) * sub + within


def _attn_a_meta_kernel(lam_ref, q_ref, km_ref, vm_ref, gate_ref, g_ref, o_ref, *, lambda_init):
    hd = A_HEAD_DIM
    q = q_ref[...]
    km = km_ref[...]
    ems, ls = [], []
    for h in range(2):
        sm = lax.dot_general(q[:, h * hd:(h + 1) * hd], km[:, h * hd:(h + 1) * hd], _NT,
                             preferred_element_type=F32)
        em = jnp.exp2(sm - jnp.max(sm, axis=-1, keepdims=True))
        ls.append(jnp.sum(em, axis=-1, keepdims=True))
        ems.append(em.astype(BF16))
    acc = jnp.dot(jnp.concatenate(ems, axis=0), vm_ref[...], preferred_element_type=F32)
    lam = _lambda_value(lam_ref, lambda_init)
    o = _diff_epilogue(acc, ls[0], ls[1], lam, g_ref[...], gate_ref[...], lambda_init)
    o_ref[...] = o.astype(o_ref.dtype)


def _attn_a_meta(qkvg_meta, lam_vecs, subln_g, *, lambda_init):
    hw = 2 * A_HEAD_DIM
    nh = A_HEADS
    kern = functools.partial(_attn_a_meta_kernel, lambda_init=lambda_init)
    return pl.pallas_call(
        kern,
        grid=(nh,),
        in_specs=[
            pl.BlockSpec((4, A_HEAD_DIM), lambda h: (0, 0)),
            pl.BlockSpec((META_ROWS, hw), lambda h: (0, h)),
            pl.BlockSpec((N_META, hw), lambda h: (0, nh + h)),
            pl.BlockSpec((N_META, hw), lambda h: (0, 2 * nh + h)),
            pl.BlockSpec((META_ROWS, hw), lambda h: (0, 3 * nh + h)),
            pl.BlockSpec((1, hw), lambda h: (0, 0)),
        ],
        out_specs=pl.BlockSpec((META_ROWS, hw), lambda h: (0, h)),
        out_shape=jax.ShapeDtypeStruct((META_ROWS, D_MODEL), BF16),
        compiler_params=_params("parallel"),
        name="diff_attn_meta",
    )(lam_vecs, qkvg_meta, qkvg_meta, qkvg_meta, qkvg_meta, subln_g.reshape(1, hw))


SWA_KEYS = 256
SWA_SINK_COL = N_META


def _swa_kernel(sink_ref, q_ref, kd_ref, vd_ref, kdm_ref, vdm_ref, gate_ref, o_ref, *, n_chunks):
    step = pl.program_id(1)
    tq = CHUNK
    win = 3 * CHUNK
    nk = SWA_KEYS
    neg_inf = jnp.float32(-jnp.inf)

    colh = lax.broadcasted_iota(jnp.int32, (1, 2 * nk), 1) & (nk - 1)
    lane = lax.broadcasted_iota(jnp.int32, (nk, LANES), 1)
    lo_mask = jnp.where(lane < B_HEAD_DIM, 1.0, 0.0).astype(BF16)
    hi_mask = jnp.where(lane < B_HEAD_DIM, 0.0, 1.0).astype(BF16)
    pad = jnp.zeros((CHUNK - N_META, LANES), BF16)
    indicator = jnp.concatenate([lo_mask, hi_mask], axis=0)
    sink_rows = sink_ref[...] * LOG2E

    def block_diag(meta_rows, win_rows):
        w = jnp.concatenate([meta_rows, pad, win_rows], axis=0)
        return jnp.concatenate([w * lo_mask, w * hi_mask], axis=0)

    def chunk_body(cc, carry):
        c = step * n_chunks + cc
        r0 = pl.multiple_of(cc * CHUNK, CHUNK)
        start = pl.multiple_of(jnp.maximum(c * CHUNK - 2 * CHUNK, 0), CHUNK)
        n_vis = (jnp.minimum(c, 2) + 1) * CHUNK
        valid = jnp.logical_or(colh <= SWA_SINK_COL,
                               jnp.logical_and(colh >= CHUNK, colh < CHUNK + n_vis))
        for kv in range(B_KV_HEADS):
            cs = slice(kv * LANES, (kv + 1) * LANES)
            kbd = block_diag(kdm_ref[:, cs], kd_ref[pl.ds(start, win), cs])
            vbd = jnp.concatenate([block_diag(vdm_ref[:, cs], vd_ref[pl.ds(start, win), cs]),
                                   indicator], axis=1)
            q4 = jnp.concatenate(
                [q_ref[pl.ds(r0, tq), kv * 512 + p * LANES: kv * 512 + (p + 1) * LANES]
                 for p in range(4)], axis=0)
            s = lax.dot_general(q4, kbd, _NT, preferred_element_type=F32)
            p_rows = []
            for p in range(4):
                bias = jnp.where(valid, sink_rows[kv * 4 + p:kv * 4 + p + 1, :], neg_inf)
                sp = s[p * tq:(p + 1) * tq] + bias
                p_cols = []
                for hp in range(2):
                    sq = sp[:, hp * nk:(hp + 1) * nk]
                    e = jnp.exp2(sq - jnp.max(sq, axis=-1, keepdims=True))
                    p_cols.append(e.astype(BF16))
                p_rows.append(jnp.concatenate(p_cols, axis=1))
            pmat = jnp.concatenate(p_rows, axis=0)
            o4 = jnp.dot(pmat, vbd, preferred_element_type=F32)
            on = o4[:, :LANES] * (1.0 / o4[:, LANES:])
            for p in range(4):
                osl = slice(kv * 512 + p * LANES, kv * 512 + (p + 1) * LANES)
                gate = gate_ref[pl.ds(r0, tq), osl].astype(F32)
                o_ref[pl.ds(r0, tq), osl] = (on[p * tq:(p + 1) * tq]
                                             * (gate * jax.nn.sigmoid(gate))).astype(o_ref.dtype)
        return carry

    lax.fori_loop(0, n_chunks, chunk_body, 0, unroll=4)


def _swa(qg, kvd, kvd_meta, sinks, *, batch, seq, n_chunks):
    ns = seq // (CHUNK * n_chunks)
    rows = CHUNK * n_chunks
    kw = B_KV_HEADS * LANES
    sink_rows = jnp.zeros((B_KV_HEADS * 4, 2, SWA_KEYS), F32).at[:, :, SWA_SINK_COL].set(
        sinks.astype(F32).reshape(B_KV_HEADS * 4, 2)).reshape(B_KV_HEADS * 4, 2 * SWA_KEYS)
    return pl.pallas_call(
        functools.partial(_swa_kernel, n_chunks=n_chunks),
        grid=(batch, ns),
        in_specs=[
            pl.BlockSpec((B_KV_HEADS * 4, 2 * SWA_KEYS), lambda b, c: (0, 0)),
            pl.BlockSpec((rows, D_MODEL), lambda b, c: (b * ns + c, 0)),
            pl.BlockSpec((seq, kw), lambda b, c: (b, 0)),
            pl.BlockSpec((seq, kw), lambda b, c: (b, 1)),
            pl.BlockSpec((N_META, kw), lambda b, c: (0, 0)),
            pl.BlockSpec((N_META, kw), lambda b, c: (0, 1)),
            pl.BlockSpec((rows, D_MODEL), lambda b, c: (b * ns + c, 1)),
        ],
        out_specs=pl.BlockSpec((rows, D_MODEL), lambda b, c: (b * ns + c, 0)),
        out_shape=jax.ShapeDtypeStruct((batch * seq, D_MODEL), BF16),
        compiler_params=_params("parallel", "arbitrary"),
        name="swa_attn",
    )(sink_rows, qg, kvd, kvd, kvd_meta, kvd_meta, qg)


def _out_kernel(*refs, n_norm, emit_h):
    o_ref, w_ref, h_ref = refs[:3]
    g_refs = refs[3:3 + n_norm]
    outs = refs[3 + n_norm:]
    tm = o_ref.shape[0]
    sub = min(tm, PROJ_SUB_ROWS)
    w = w_ref[...]
    for rb in range(tm // sub):
        rows = slice(rb * sub, (rb + 1) * sub)
        acc = jnp.dot(o_ref[rows, :], w, preferred_element_type=F32) + h_ref[rows, :]
        norm_outs = outs
        if emit_h:
            outs[0][rows, :] = acc
            norm_outs = outs[1:]
        r = lax.rsqrt(jnp.mean(acc * acc, axis=-1, keepdims=True) + NORM_EPS)
        xn = acc * r
        for g_ref, n_ref in zip(g_refs, norm_outs):
            n_ref[rows, :] = (xn * g_ref[...]).astype(n_ref.dtype)


def _out_proj(o, w, h, gains, *, tm, emit_h, norm_dtype, o_row_block=None):
    m = h.shape[0]
    row = pl.BlockSpec((tm, D_MODEL), lambda i: (i, 0))
    o_spec = row if o_row_block is None else pl.BlockSpec((tm, D_MODEL), lambda i: (o_row_block(i), 0))
    gspec = pl.BlockSpec((1, D_MODEL), lambda i: (0, 0))
    out_shape = ([jax.ShapeDtypeStruct((m, D_MODEL), F32)] if emit_h else []) + [
        jax.ShapeDtypeStruct((m, D_MODEL), norm_dtype) for _ in gains]
    kern = functools.partial(_out_kernel, n_norm=len(gains), emit_h=emit_h)
    return pl.pallas_call(
        kern,
        grid=(m // tm,),
        in_specs=[o_spec, pl.BlockSpec((D_MODEL, D_MODEL), lambda i: (0, 0)), row] + [gspec] * len(gains),
        out_specs=[row] * len(out_shape),
        out_shape=out_shape,
        compiler_params=_params("parallel"),
        name="out_proj",
    )(o, w, h, *[g.reshape(1, D_MODEL) for g in gains])


def kernel(x, meta_tokens, a_norm, a_w_in, a_w_out, a_lambda_q1, a_lambda_k1, a_lambda_q2,
           a_lambda_k2, a_subln, kv_norm, w_kv, b_norm, b_w_in, b_w_out, b_sinks, final_norm):
    batch, seq, d = x.shape
    n_a = a_w_in.shape[0]
    n_b = b_w_in.shape[0]
    h_r = x.reshape(batch * seq, d)
    h_m = jnp.concatenate([meta_tokens.astype(x.dtype),
                           jnp.zeros((META_ROWS - N_META, d), x.dtype)], axis=0)

    tm_r = 1024
    tm_out = 512
    tq_a = 512
    a_scale = A_HEAD_DIM ** -0.5 * LOG2E
    b_scale = B_HEAD_DIM ** -0.5 * LOG2E
    attn_rows = functools.partial(_attn_a_row_block, batch=batch, seq=seq, tq=tq_a, tm=tm_out)

    pos_r = jnp.arange(seq, dtype=jnp.int32) + N_META
    pos_m = jnp.arange(META_ROWS, dtype=jnp.int32)
    tab_a_r = _proj_tables(pos_r, A_HEAD_DIM, A_ROT_HALF, a_scale, tm_r)
    tab_a_m = _proj_tables(pos_m, A_HEAD_DIM, A_ROT_HALF, a_scale, META_ROWS)
    tab_b_r = _proj_tables(pos_r, B_HEAD_DIM, B_ROT_HALF, b_scale, tm_r)
    tab_b_m = _proj_tables(pos_m, B_HEAD_DIM, B_ROT_HALF, b_scale, META_ROWS)

    wk, wv = jnp.split(w_kv, 2, axis=1)

    def dup_heads(w):
        w = w.reshape(d, B_KV_HEADS, 1, B_HEAD_DIM)
        return jnp.broadcast_to(w, (d, B_KV_HEADS, 2, B_HEAD_DIM)).reshape(d, B_KV_HEADS * LANES)

    w_kvd = jnp.concatenate([dup_heads(wk), dup_heads(wv)], axis=1)

    hn_r = _rms_norm(h_r, a_norm[0], 512)
    hn_m = _rms_norm(h_m, a_norm[0], META_ROWS)
    for l in range(n_a):
        lambda_init = 0.8 - 0.6 * math.exp(-0.3 * l)
        proj_kw = dict(tn=1024, rope_cols=2 * d, q_cols=d, half=A_ROT_HALF)
        w_out = a_w_out[l].astype(BF16)
        qkvg_r = _proj(hn_r, a_w_in, l, tab_a_r, tm=tm_r, **proj_kw)
        qkvg_m = _proj(hn_m, a_w_in, l, tab_a_m, tm=META_ROWS, **proj_kw)
        lam_vecs = jnp.stack([a_lambda_q1[l], a_lambda_k1[l], a_lambda_q2[l], a_lambda_k2[l]]).astype(F32)
        o_r = _attn_a(qkvg_r, qkvg_m, lam_vecs, a_subln[l], batch=batch, seq=seq, tq=tq_a,
                      lambda_init=lambda_init)
        o_m = _attn_a_meta(qkvg_m, lam_vecs, a_subln[l], lambda_init=lambda_init)
        if l + 1 < n_a:
            gains_r = gains_m = [a_norm[l + 1]]
        else:
            gains_r, gains_m = [kv_norm, b_norm[0]], [kv_norm]
        outs_r = _out_proj(o_r.reshape(batch * seq, d), w_out, h_r, gains_r, tm=tm_out,
                           emit_h=True, norm_dtype=BF16, o_row_block=attn_rows)
        outs_m = _out_proj(o_m, w_out, h_m, gains_m, tm=META_ROWS, emit_h=True, norm_dtype=BF16)
        h_r, h_m = outs_r[0], outs_m[0]
        hn_r, hn_m = outs_r[1], outs_m[1]

    kv_kw = dict(tn=512, rope_cols=B_KV_HEADS * LANES, q_cols=0, half=B_ROT_HALF)
    kvd_r = _proj(hn_r, w_kvd[None], 0, tab_b_r, tm=tm_r, **kv_kw)
    kvd_m = _proj(hn_m, w_kvd[None], 0, tab_b_m, tm=META_ROWS, **kv_kw)
    hn_r = outs_r[2]

    for l in range(n_b):
        qg = _proj(hn_r, b_w_in, l, tab_b_r, tm=tm_r, tn=1024, rope_cols=d, q_cols=d,
                   half=B_ROT_HALF)
        o_r = _swa(qg, kvd_r, kvd_m, b_sinks[l], batch=batch, seq=seq, n_chunks=4)
        w_out = b_w_out[l].astype(BF16)
        if l + 1 < n_b:
            h_r, hn_r = _out_proj(o_r, w_out, h_r, [b_norm[l + 1]], tm=tm_out, emit_h=True,
                                  norm_dtype=BF16)
        else:
            (out,) = _out_proj(o_r, w_out, h_r, [final_norm], tm=tm_out, emit_h=False,
                               norm_dtype=x.dtype)
    return out.reshape(batch, seq, d)
```

```python
import functools
import math

import jax
import jax.numpy as jnp
from jax import lax
from jax.experimental import pallas as pl
from jax.experimental.pallas import tpu as pltpu

D_MODEL = 2048
N_META = 16
META_ROWS = 64
CHUNK = 64
CHUNK_SHIFT = 6
ROPE_THETA = 500000.0
NORM_EPS = 1e-5
A_HEADS = 8
A_HEAD_DIM = 128
A_ROT_HALF = 16
B_HEAD_DIM = 64
B_KV_HEADS = 4
B_ROT_HALF = 8
LANES = 128
VMEM_LIMIT = 56 * 1024 * 1024
LOG2E = 1.4426950408889634

F32 = jnp.float32
BF16 = jnp.bfloat16
_NT = (((1,), (1,)), ((), ()))


def _params(*sem):
    return pltpu.CompilerParams(dimension_semantics=sem, vmem_limit_bytes=VMEM_LIMIT)


def _norm_kernel(x_ref, g_ref, o_ref):
    x = x_ref[...]
    r = lax.rsqrt(jnp.mean(x * x, axis=-1, keepdims=True) + NORM_EPS)
    o_ref[...] = (x * r * g_ref[...]).astype(o_ref.dtype)


def _rms_norm(x, g, tm):
    m = x.shape[0]
    return pl.pallas_call(
        _norm_kernel,
        grid=(m // tm,),
        in_specs=[pl.BlockSpec((tm, D_MODEL), lambda i: (i, 0)),
                  pl.BlockSpec((1, D_MODEL), lambda i: (0, 0))],
        out_specs=pl.BlockSpec((tm, D_MODEL), lambda i: (i, 0)),
        out_shape=jax.ShapeDtypeStruct((m, D_MODEL), BF16),
        compiler_params=_params("parallel"),
        name="rms_norm",
    )(x, g.reshape(1, D_MODEL))


PROJ_SUB_ROWS = 128
DIAG_PIECES = 4


def _proj_kernel(a_ref, w_ref, c_ref, s1_ref, s2_ref, o_ref, wb_ref, *, half):
    tm, tn = o_ref.shape
    sub = min(tm, PROJ_SUB_ROWS)

    @pl.when(pl.program_id(1) == 0)
    def _():
        wb_ref[...] = w_ref[...].astype(wb_ref.dtype)

    w = wb_ref[...]
    for rb in range(tm // sub):
        rows = slice(rb * sub, (rb + 1) * sub)
        acc = jnp.dot(a_ref[rows, :], w, preferred_element_type=F32)
        c, s1, s2 = c_ref[rows, :], s1_ref[rows, :], s2_ref[rows, :]
        for cb in range(tn // LANES):
            x = acc[:, cb * LANES:(cb + 1) * LANES]
            y = (x * c + pltpu.roll(x, LANES - half, 1) * s1 + pltpu.roll(x, half, 1) * s2)
            o_ref[rows, cb * LANES:(cb + 1) * LANES] = y.astype(o_ref.dtype)


def _proj(a, w, layer, tables, *, tm, tn, rope_cols, q_cols, half):
    m, n = a.shape[0], w.shape[2]
    tab_blocks = (tables[0].shape[0] - tm) // (2 * tm)
    n_q_tiles, n_rope_tiles = q_cols // tn, rope_cols // tn

    def tab_index(j, i):
        pos = i % tab_blocks
        return (jnp.where(j < n_q_tiles, pos,
                          jnp.where(j < n_rope_tiles, tab_blocks + pos, 2 * tab_blocks)), 0)

    tab_spec = pl.BlockSpec((tm, LANES), tab_index)
    return pl.pallas_call(
        functools.partial(_proj_kernel, half=half),
        grid=(n // tn, m // tm),
        in_specs=[pl.BlockSpec((tm, D_MODEL), lambda j, i: (i, 0)),
                  pl.BlockSpec((None, D_MODEL, tn), lambda j, i: (layer, 0, j)),
                  tab_spec, tab_spec, tab_spec],
        out_specs=pl.BlockSpec((tm, tn), lambda j, i: (i, j)),
        out_shape=jax.ShapeDtypeStruct((m, n), BF16),
        scratch_shapes=[pltpu.VMEM((D_MODEL, tn), BF16)],
        compiler_params=_params("arbitrary", "arbitrary"),
        name="proj_rope",
    )(a, w, *tables)


def _proj_tables(pos, head_dim, half, q_scale, tm):
    ident = (jnp.ones((tm, LANES), F32), jnp.zeros((tm, LANES), F32), jnp.zeros((tm, LANES), F32))
    return tuple(jnp.concatenate([t * q_scale, t, e], axis=0)
                 for t, e in zip(_rope_tables(pos, head_dim, half), ident))


def _rope_tables(pos, head_dim, half):
    inv = ROPE_THETA ** (-jnp.arange(0, 2 * half, 2, dtype=F32) / (2 * half))
    ang = pos.astype(F32)[:, None] * inv[None, :]
    cos, sin = jnp.cos(ang), jnp.sin(ang)
    p = pos.shape[0]
    ones = jnp.ones((p, head_dim - 2 * half), F32)
    zeros = jnp.zeros((p, head_dim - 2 * half), F32)
    zh = jnp.zeros((p, half), F32)
    c = jnp.concatenate([cos, cos, ones], axis=1)
    s1 = jnp.concatenate([-sin, zh, zeros], axis=1)
    s2 = jnp.concatenate([zh, sin, zeros], axis=1)
    rep = LANES // head_dim
    return tuple(jnp.tile(t, (1, rep)) for t in (c, s1, s2))


def _lambda_value(lam_ref, lambda_init):
    lv = lam_ref[...]
    return (jnp.exp(jnp.sum(lv[0:1] * lv[1:2], axis=-1, keepdims=True))
            - jnp.exp(jnp.sum(lv[2:3] * lv[3:4], axis=-1, keepdims=True)) + lambda_init)


def _diff_epilogue(acc, l1, l2, lam, g, gate, lambda_init):
    tq = acc.shape[0] // 2
    o = acc[:tq] * (1.0 / l1) - lam * (acc[tq:] * (1.0 / l2))
    r = lax.rsqrt(jnp.mean(o * o, axis=-1, keepdims=True) + NORM_EPS)
    o = o * r * g * (1.0 - lambda_init)
    gate = gate.astype(F32)
    return o * (gate * jax.nn.sigmoid(gate))


def _lane_block_reduce(x, op):
    r = x[:, :LANES]
    for c in range(1, x.shape[1] // LANES):
        r = op(r, x[:, c * LANES:(c + 1) * LANES])
    return r


def _attn_a_kernel(lam_ref, dmat_ref, qa_ref, qb_ref, k_ref, v_ref, km_ref, vm_ref, ga_ref, gb_ref,
                   g_ref, o_ref, q_sc, s_ref, mr_ref, mb_ref, ls_ref, acc_ref, *, tq, n_tiles,
                   lambda_init):
    i = pl.program_id(2)
    hd = A_HEAD_DIM
    n_steps = n_tiles + 1
    hq = tq // DIAG_PIECES
    neg_inf = jnp.float32(-jnp.inf)
    q_sc[0] = qa_ref[...]
    q_sc[1] = qb_ref[...]
    km = km_ref[...]
    vm = vm_ref[...]
    lam = _lambda_value(lam_ref, lambda_init)
    vis = dmat_ref[...] <= 0

    def step_info(t):
        if t == 0:
            return 0, i, True
        if t == n_steps - 1:
            return 1, n_tiles - 1 - i, True
        if t > n_tiles // 2 - 1:
            return 1, t - i - 1, False
        low = t <= i
        return jnp.where(low, 0, 1), jnp.where(low, i - t, t - i - 1), False

    def parts(diag):
        if not diag:
            return ((slice(0, tq), tq),)
        return tuple((slice(r * hq, (r + 1) * hq), (r + 1) * hq) for r in range(DIAG_PIECES))

    init = jnp.full((tq, LANES), neg_inf, F32)
    for h in range(2):
        for sel in range(2):
            mr_ref[h, sel] = init
            ls_ref[h, sel] = jnp.zeros((tq, LANES), F32)
    acc_ref[...] = jnp.zeros(acc_ref.shape, F32)
    for t in range(n_steps):
        sel, kt, diag = step_info(t)
        krows = k_ref[pl.ds(pl.multiple_of(kt * tq, tq), tq), :]
        for h in range(2):
            for rows, nk in parts(diag):
                s = lax.dot_general(q_sc[sel, rows, h * hd:(h + 1) * hd],
                                    krows[:nk, h * hd:(h + 1) * hd], _NT,
                                    preferred_element_type=F32)
                if diag:
                    masked = jnp.where(vis, s[:, nk - hq:], neg_inf)
                    s = masked if nk == hq else jnp.concatenate([s[:, :nk - hq], masked], axis=1)
                s_ref[h, t, rows, :nk] = s
                mr_ref[h, sel, rows, :] = jnp.maximum(mr_ref[h, sel, rows, :],
                                                      _lane_block_reduce(s, jnp.maximum))

    def meta_scores(sel, h):
        return lax.dot_general(q_sc[sel][:, h * hd:(h + 1) * hd], km[:, h * hd:(h + 1) * hd], _NT,
                               preferred_element_type=F32)

    for sel in range(2):
        for h in range(2):
            m = jnp.maximum(jnp.max(meta_scores(sel, h), axis=-1, keepdims=True),
                            jnp.max(mr_ref[h, sel], axis=-1, keepdims=True))
            mb_ref[h, sel] = jnp.broadcast_to(m, (tq, LANES))

    for t in range(n_steps):
        sel, kt, diag = step_info(t)
        vrows = v_ref[pl.ds(pl.multiple_of(kt * tq, tq), tq), :]
        for rows, nk in parts(diag):
            ps = []
            for h in range(2):
                mb = mb_ref[h, sel, rows, :]
                blocks = []
                ls = None
                for c in range(nk // LANES):
                    e = jnp.exp2(s_ref[h, t, rows, c * LANES:(c + 1) * LANES] - mb)
                    ls = e if ls is None else ls + e
                    blocks.append(e.astype(BF16))
                ls_ref[h, sel, rows, :] = ls_ref[h, sel, rows, :] + ls
                ps.append(jnp.concatenate(blocks, axis=1))
            pv = jnp.dot(jnp.concatenate(ps, axis=0), vrows[:nk], preferred_element_type=F32)
            n = rows.stop - rows.start
            for h in range(2):
                arows = slice(h * tq + rows.start, h * tq + rows.stop)
                acc_ref[sel, arows, :] = acc_ref[sel, arows, :] + pv[h * n:(h + 1) * n]

    for sel, gate_ref in ((0, ga_ref), (1, gb_ref)):
        ems, ls = [], []
        for h in range(2):
            em = jnp.exp2(meta_scores(sel, h) - mb_ref[h, sel, :, :1])
            ls.append(jnp.sum(ls_ref[h, sel], axis=-1, keepdims=True)
                      + jnp.sum(em, axis=-1, keepdims=True))
            ems.append(em.astype(BF16))
        acc = acc_ref[sel] + jnp.dot(jnp.concatenate(ems, axis=0), vm, preferred_element_type=F32)
        o = _diff_epilogue(acc, ls[0], ls[1], lam, g_ref[...], gate_ref[...], lambda_init)
        o_ref[sel] = o.astype(o_ref.dtype)


def _attn_a(qkvg, qkvg_meta, lam_vecs, subln_g, *, batch, seq, tq, lambda_init):
    n_tiles = seq // tq
    half = n_tiles // 2
    hw = 2 * A_HEAD_DIM
    nh = A_HEADS
    piece = tq // DIAG_PIECES
    idx = jnp.arange(piece, dtype=jnp.int32) >> CHUNK_SHIFT
    dmat = idx[None, :] - idx[:, None]
    kern = functools.partial(_attn_a_kernel, tq=tq, n_tiles=n_tiles, lambda_init=lambda_init)
    lo = lambda b, h, i: b * n_tiles + i
    hi = lambda b, h, i: b * n_tiles + n_tiles - 1 - i
    return pl.pallas_call(
        kern,
        grid=(batch, nh, half),
        in_specs=[
            pl.BlockSpec((4, A_HEAD_DIM), lambda b, h, i: (0, 0)),
            pl.BlockSpec((piece, piece), lambda b, h, i: (0, 0)),
            pl.BlockSpec((tq, hw), lambda b, h, i: (lo(b, h, i), h)),
            pl.BlockSpec((tq, hw), lambda b, h, i: (hi(b, h, i), h)),
            pl.BlockSpec((seq, hw), lambda b, h, i: (b, nh + h)),
            pl.BlockSpec((seq, hw), lambda b, h, i: (b, 2 * nh + h)),
            pl.BlockSpec((N_META, hw), lambda b, h, i: (0, nh + h)),
            pl.BlockSpec((N_META, hw), lambda b, h, i: (0, 2 * nh + h)),
            pl.BlockSpec((tq, hw), lambda b, h, i: (lo(b, h, i), 3 * nh + h)),
            pl.BlockSpec((tq, hw), lambda b, h, i: (hi(b, h, i), 3 * nh + h)),
            pl.BlockSpec((1, hw), lambda b, h, i: (0, 0)),
        ],
        out_specs=pl.BlockSpec((2, None, None, tq, hw), lambda b, h, i: (0, b, i, 0, h)),
        out_shape=jax.ShapeDtypeStruct((2, batch, half, tq, D_MODEL), BF16),
        scratch_shapes=[pltpu.VMEM((2, tq, hw), BF16),
                        pltpu.VMEM((2, n_tiles + 1, tq, tq), F32),
                        pltpu.VMEM((2, 2, tq, LANES), F32),
                        pltpu.VMEM((2, 2, tq, LANES), F32),
                        pltpu.VMEM((2, 2, tq, LANES), F32),
                        pltpu.VMEM((2, 2 * tq, hw), F32)],
        compiler_params=_params("parallel", "parallel", "arbitrary"),
        name="diff_attn",
    )(lam_vecs, dmat, qkvg, qkvg, qkvg, qkvg, qkvg_meta, qkvg_meta, qkvg, qkvg,
      subln_g.reshape(1, hw))


def _attn_a_row_block(r, *, batch, seq, tq, tm):
    n_tiles = seq // tq
    half = n_tiles // 2
    sub = tq // tm
    b = r // (n_tiles * sub)
    u = r % (n_tiles * sub)
    tile, within = u // sub, u % sub
    hi = tile // half
    pos = jnp.where(hi == 0, tile, n_tiles - 1 - tile)
    return ((hi * batch + b) * half + pos) * sub + within


def _attn_a_meta_kernel(lam_ref, q_ref, km_ref, vm_ref, gate_ref, g_ref, o_ref, *, lambda_init):
    hd = A_HEAD_DIM
    q = q_ref[...]
    km = km_ref[...]
    ems, ls = [], []
    for h in range(2):
        sm = lax.dot_general(q[:, h * hd:(h + 1) * hd], km[:, h * hd:(h + 1) * hd], _NT,
                             preferred_element_type=F32)
        em = jnp.exp2(sm - jnp.max(sm, axis=-1, keepdims=True))
        ls.append(jnp.sum(em, axis=-1, keepdims=True))
        ems.append(em.astype(BF16))
    acc = jnp.dot(jnp.concatenate(ems, axis=0), vm_ref[...], preferred_element_type=F32)
    lam = _lambda_value(lam_ref, lambda_init)
    o = _diff_epilogue(acc, ls[0], ls[1], lam, g_ref[...], gate_ref[...], lambda_init)
    o_ref[...] = o.astype(o_ref.dtype)


def _attn_a_meta(qkvg_meta, lam_vecs, subln_g, *, lambda_init):
    hw = 2 * A_HEAD_DIM
    nh = A_HEADS
    kern = functools.partial(_attn_a_meta_kernel, lambda_init=lambda_init)
    return pl.pallas_call(
        kern,
        grid=(nh,),
        in_specs=[
            pl.BlockSpec((4, A_HEAD_DIM), lambda h: (0, 0)),
            pl.BlockSpec((META_ROWS, hw), lambda h: (0, h)),
            pl.BlockSpec((N_META, hw), lambda h: (0, nh + h)),
            pl.BlockSpec((N_META, hw), lambda h: (0, 2 * nh + h)),
            pl.BlockSpec((META_ROWS, hw), lambda h: (0, 3 * nh + h)),
            pl.BlockSpec((1, hw), lambda h: (0, 0)),
        ],
        out_specs=pl.BlockSpec((META_ROWS, hw), lambda h: (0, h)),
        out_shape=jax.ShapeDtypeStruct((META_ROWS, D_MODEL), BF16),
        compiler_params=_params("parallel"),
        name="diff_attn_meta",
    )(lam_vecs, qkvg_meta, qkvg_meta, qkvg_meta, qkvg_meta, subln_g.reshape(1, hw))


SWA_KEYS = 256
SWA_SINK_COL = N_META


def _swa_kernel(sink_ref, q_ref, kd_ref, vd_ref, kdm_ref, vdm_ref, gate_ref, o_ref, *, n_chunks):
    step = pl.program_id(1)
    tq = CHUNK
    win = 3 * CHUNK
    nk = SWA_KEYS
    neg_inf = jnp.float32(-jnp.inf)

    colh = lax.broadcasted_iota(jnp.int32, (1, 2 * nk), 1) & (nk - 1)
    lane = lax.broadcasted_iota(jnp.int32, (nk, LANES), 1)
    lo_mask = jnp.where(lane < B_HEAD_DIM, 1.0, 0.0).astype(BF16)
    hi_mask = jnp.where(lane < B_HEAD_DIM, 0.0, 1.0).astype(BF16)
    pad = jnp.zeros((CHUNK - N_META, LANES), BF16)
    indicator = jnp.concatenate([lo_mask, hi_mask], axis=0)
    sink_rows = sink_ref[...] * LOG2E

    def block_diag(meta_rows, win_rows):
        w = jnp.concatenate([meta_rows, pad, win_rows], axis=0)
        return jnp.concatenate([w * lo_mask, w * hi_mask], axis=0)

    def chunk_body(cc, carry):
        c = step * n_chunks + cc
        r0 = pl.multiple_of(cc * CHUNK, CHUNK)
        start = pl.multiple_of(jnp.maximum(c * CHUNK - 2 * CHUNK, 0), CHUNK)
        n_vis = (jnp.minimum(c, 2) + 1) * CHUNK
        valid = jnp.logical_or(colh <= SWA_SINK_COL,
                               jnp.logical_and(colh >= CHUNK, colh < CHUNK + n_vis))
        for kv in range(B_KV_HEADS):
            cs = slice(kv * LANES, (kv + 1) * LANES)
            kbd = block_diag(kdm_ref[:, cs], kd_ref[pl.ds(start, win), cs])
            vbd = jnp.concatenate([block_diag(vdm_ref[:, cs], vd_ref[pl.ds(start, win), cs]),
                                   indicator], axis=1)
            q4 = jnp.concatenate(
                [q_ref[pl.ds(r0, tq), kv * 512 + p * LANES: kv * 512 + (p + 1) * LANES]
                 for p in range(4)], axis=0)
            s = lax.dot_general(q4, kbd, _NT, preferred_element_type=F32)
            p_rows = []
            for p in range(4):
                bias = jnp.where(valid, sink_rows[kv * 4 + p:kv * 4 + p + 1, :], neg_inf)
                sp = s[p * tq:(p + 1) * tq] + bias
                p_cols = []
                for hp in range(2):
                    sq = sp[:, hp * nk:(hp + 1) * nk]
                    e = jnp.exp2(sq - jnp.max(sq, axis=-1, keepdims=True))
                    p_cols.append(e.astype(BF16))
                p_rows.append(jnp.concatenate(p_cols, axis=1))
            pmat = jnp.concatenate(p_rows, axis=0)
            o4 = jnp.dot(pmat, vbd, preferred_element_type=F32)
            on = o4[:, :LANES] * (1.0 / o4[:, LANES:])
            for p in range(4):
                osl = slice(kv * 512 + p * LANES, kv * 512 + (p + 1) * LANES)
                gate = gate_ref[pl.ds(r0, tq), osl].astype(F32)
                o_ref[pl.ds(r0, tq), osl] = (on[p * tq:(p + 1) * tq]
                                             * (gate * jax.nn.sigmoid(gate))).astype(o_ref.dtype)
        return carry

    lax.fori_loop(0, n_chunks, chunk_body, 0, unroll=4)


def _swa(qg, kvd, kvd_meta, sinks, *, batch, seq, n_chunks):
    ns = seq // (CHUNK * n_chunks)
    rows = CHUNK * n_chunks
    kw = B_KV_HEADS * LANES
    sink_rows = jnp.zeros((B_KV_HEADS * 4, 2, SWA_KEYS), F32).at[:, :, SWA_SINK_COL].set(
        sinks.astype(F32).reshape(B_KV_HEADS * 4, 2)).reshape(B_KV_HEADS * 4, 2 * SWA_KEYS)
    return pl.pallas_call(
        functools.partial(_swa_kernel, n_chunks=n_chunks),
        grid=(batch, ns),
        in_specs=[
            pl.BlockSpec((B_KV_HEADS * 4, 2 * SWA_KEYS), lambda b, c: (0, 0)),
            pl.BlockSpec((rows, D_MODEL), lambda b, c: (b * ns + c, 0)),
            pl.BlockSpec((seq, kw), lambda b, c: (b, 0)),
            pl.BlockSpec((seq, kw), lambda b, c: (b, 1)),
            pl.BlockSpec((N_META, kw), lambda b, c: (0, 0)),
            pl.BlockSpec((N_META, kw), lambda b, c: (0, 1)),
            pl.BlockSpec((rows, D_MODEL), lambda b, c: (b * ns + c, 1)),
        ],
        out_specs=pl.BlockSpec((rows, D_MODEL), lambda b, c: (b * ns + c, 0)),
        out_shape=jax.ShapeDtypeStruct((batch * seq, D_MODEL), BF16),
        compiler_params=_params("parallel", "arbitrary"),
        name="swa_attn",
    )(sink_rows, qg, kvd, kvd, kvd_meta, kvd_meta, qg)


def _out_kernel(*refs, n_norm, emit_h):
    o_ref, w_ref, h_ref = refs[:3]
    g_refs = refs[3:3 + n_norm]
    outs = refs[3 + n_norm:]
    tm = o_ref.shape[0]
    sub = min(tm, PROJ_SUB_ROWS)
    w = w_ref[...]
    for rb in range(tm // sub):
        rows = slice(rb * sub, (rb + 1) * sub)
        acc = jnp.dot(o_ref[rows, :], w, preferred_element_type=F32) + h_ref[rows, :]
        norm_outs = outs
        if emit_h:
            outs[0][rows, :] = acc
            norm_outs = outs[1:]
        r = lax.rsqrt(jnp.mean(acc * acc, axis=-1, keepdims=True) + NORM_EPS)
        xn = acc * r
        for g_ref, n_ref in zip(g_refs, norm_outs):
            n_ref[rows, :] = (xn * g_ref[...]).astype(n_ref.dtype)


def _out_proj(o, w, h, gains, *, tm, emit_h, norm_dtype, o_row_block=None):
    m = h.shape[0]
    row = pl.BlockSpec((tm, D_MODEL), lambda i: (i, 0))
    o_spec = row if o_row_block is None else pl.BlockSpec((tm, D_MODEL), lambda i: (o_row_block(i), 0))
    gspec = pl.BlockSpec((1, D_MODEL), lambda i: (0, 0))
    out_shape = ([jax.ShapeDtypeStruct((m, D_MODEL), F32)] if emit_h else []) + [
        jax.ShapeDtypeStruct((m, D_MODEL), norm_dtype) for _ in gains]
    kern = functools.partial(_out_kernel, n_norm=len(gains), emit_h=emit_h)
    return pl.pallas_call(
        kern,
        grid=(m // tm,),
        in_specs=[o_spec, pl.BlockSpec((D_MODEL, D_MODEL), lambda i: (0, 0)), row] + [gspec] * len(gains),
        out_specs=[row] * len(out_shape),
        out_shape=out_shape,
        compiler_params=_params("parallel"),
        name="out_proj",
    )(o, w, h, *[g.reshape(1, D_MODEL) for g in gains])


def kernel(x, meta_tokens, a_norm, a_w_in, a_w_out, a_lambda_q1, a_lambda_k1, a_lambda_q2,
           a_lambda_k2, a_subln, kv_norm, w_kv, b_norm, b_w_in, b_w_out, b_sinks, final_norm):
    batch, seq, d = x.shape
    n_a = a_w_in.shape[0]
    n_b = b_w_in.shape[0]
    h_r = x.reshape(batch * seq, d)
    h_m = jnp.concatenate([meta_tokens.astype(x.dtype),
                           jnp.zeros((META_ROWS - N_META, d), x.dtype)], axis=0)

    tm_r = 1024
    tm_out = 512
    tq_a = 512
    a_scale = A_HEAD_DIM ** -0.5 * LOG2E
    b_scale = B_HEAD_DIM ** -0.5 * LOG2E
    attn_rows = functools.partial(_attn_a_row_block, batch=batch, seq=seq, tq=tq_a, tm=tm_out)

    pos_r = jnp.arange(seq, dtype=jnp.int32) + N_META
    pos_m = jnp.arange(META_ROWS, dtype=jnp.int32)
    tab_a_r = _proj_tables(pos_r, A_HEAD_DIM, A_ROT_HALF, a_scale, tm_r)
    tab_a_m = _proj_tables(pos_m, A_HEAD_DIM, A_ROT_HALF, a_scale, META_ROWS)
    tab_b_r = _proj_tables(pos_r, B_HEAD_DIM, B_ROT_HALF, b_scale, tm_r)
    tab_b_m = _proj_tables(pos_m, B_HEAD_DIM, B_ROT_HALF, b_scale, META_ROWS)

    wk, wv = jnp.split(w_kv, 2, axis=1)

    def dup_heads(w):
        w = w.reshape(d, B_KV_HEADS, 1, B_HEAD_DIM)
        return jnp.broadcast_to(w, (d, B_KV_HEADS, 2, B_HEAD_DIM)).reshape(d, B_KV_HEADS * LANES)

    w_kvd = jnp.concatenate([dup_heads(wk), dup_heads(wv)], axis=1)

    hn_r = _rms_norm(h_r, a_norm[0], 512)
    hn_m = _rms_norm(h_m, a_norm[0], META_ROWS)
    for l in range(n_a):
        lambda_init = 0.8 - 0.6 * math.exp(-0.3 * l)
        proj_kw = dict(tn=1024, rope_cols=2 * d, q_cols=d, half=A_ROT_HALF)
        w_out = a_w_out[l].astype(BF16)
        qkvg_r = _proj(hn_r, a_w_in, l, tab_a_r, tm=tm_r, **proj_kw)
        qkvg_m = _proj(hn_m, a_w_in, l, tab_a_m, tm=META_ROWS, **proj_kw)
        lam_vecs = jnp.stack([a_lambda_q1[l], a_lambda_k1[l], a_lambda_q2[l], a_lambda_k2[l]]).astype(F32)
        o_r = _attn_a(qkvg_r, qkvg_m, lam_vecs, a_subln[l], batch=batch, seq=seq, tq=tq_a,
                      lambda_init=lambda_init)
        o_m = _attn_a_meta(qkvg_m, lam_vecs, a_subln[l], lambda_init=lambda_init)
        if l + 1 < n_a:
            gains_r = gains_m = [a_norm[l + 1]]
        else:
            gains_r, gains_m = [kv_norm, b_norm[0]], [kv_norm]
        outs_r = _out_proj(o_r.reshape(batch * seq, d), w_out, h_r, gains_r, tm=tm_out,
                           emit_h=True, norm_dtype=BF16, o_row_block=attn_rows)
        outs_m = _out_proj(o_m, w_out, h_m, gains_m, tm=META_ROWS, emit_h=True, norm_dtype=BF16)
        h_r, h_m = outs_r[0], outs_m[0]
        hn_r, hn_m = outs_r[1], outs_m[1]

    kv_kw = dict(tn=512, rope_cols=B_KV_HEADS * LANES, q_cols=0, half=B_ROT_HALF)
    kvd_r = _proj(hn_r, w_kvd[None], 0, tab_b_r, tm=tm_r, **kv_kw)
    kvd_m = _proj(hn_m, w_kvd[None], 0, tab_b_m, tm=META_ROWS, **kv_kw)
    hn_r = outs_r[2]

    for l in range(n_b):
        qg = _proj(hn_r, b_w_in, l, tab_b_r, tm=tm_r, tn=1024, rope_cols=d, q_cols=d,
                   half=B_ROT_HALF)
        o_r = _swa(qg, kvd_r, kvd_m, b_sinks[l], batch=batch, seq=seq, n_chunks=4)
        w_out = b_w_out[l].astype(BF16)
        if l + 1 < n_b:
            h_r, hn_r = _out_proj(o_r, w_out, h_r, [b_norm[l + 1]], tm=tm_out, emit_h=True,
                                  norm_dtype=BF16)
        else:
            (out,) = _out_proj(o_r, w_out, h_r, [final_norm], tm=tm_out, emit_h=False,
                               norm_dtype=x.dtype)
    return out.reshape(batch, seq, d)
```

```python
import functools
import math

import jax
import jax.numpy as jnp
from jax import lax
from jax.experimental import pallas as pl
from jax.experimental.pallas import tpu as pltpu

D_MODEL = 2048
N_META = 16
META_ROWS = 64
CHUNK = 64
CHUNK_SHIFT = 6
ROPE_THETA = 500000.0
NORM_EPS = 1e-5
A_HEADS = 8
A_HEAD_DIM = 128
A_ROT_HALF = 16
B_HEAD_DIM = 64
B_KV_HEADS = 4
B_ROT_HALF = 8
LANES = 128
VMEM_LIMIT = 56 * 1024 * 1024
LOG2E = 1.4426950408889634

F32 = jnp.float32
BF16 = jnp.bfloat16
_NT = (((1,), (1,)), ((), ()))


def _params(*sem):
    return pltpu.CompilerParams(dimension_semantics=sem, vmem_limit_bytes=VMEM_LIMIT)


def _norm_kernel(x_ref, g_ref, o_ref):
    x = x_ref[...]
    r = lax.rsqrt(jnp.mean(x * x, axis=-1, keepdims=True) + NORM_EPS)
    o_ref[...] = (x * r * g_ref[...]).astype(o_ref.dtype)


def _rms_norm(x, g, tm):
    m = x.shape[0]
    return pl.pallas_call(
        _norm_kernel,
        grid=(m // tm,),
        in_specs=[pl.BlockSpec((tm, D_MODEL), lambda i: (i, 0)),
                  pl.BlockSpec((1, D_MODEL), lambda i: (0, 0))],
        out_specs=pl.BlockSpec((tm, D_MODEL), lambda i: (i, 0)),
        out_shape=jax.ShapeDtypeStruct((m, D_MODEL), BF16),
        compiler_params=_params("parallel"),
        name="rms_norm",
    )(x, g.reshape(1, D_MODEL))


PROJ_SUB_ROWS = 128
DIAG_PIECES = 4


def _proj_rows(a_ref, w, c_ref, s1_ref, s2_ref, o_ref, half):
    tm, tn = o_ref.shape
    sub = min(tm, PROJ_SUB_ROWS)
    for rb in range(tm // sub):
        rows = slice(rb * sub, (rb + 1) * sub)
        acc = jnp.dot(a_ref[rows, :], w, preferred_element_type=F32)
        c, s1, s2 = c_ref[rows, :], s1_ref[rows, :], s2_ref[rows, :]
        for cb in range(tn // LANES):
            x = acc[:, cb * LANES:(cb + 1) * LANES]
            y = (x * c + pltpu.roll(x, LANES - half, 1) * s1 + pltpu.roll(x, half, 1) * s2)
            o_ref[rows, cb * LANES:(cb + 1) * LANES] = y.astype(o_ref.dtype)


def _proj_kernel(*refs, half, with_meta):
    if with_meta:
        (a_ref, w_ref, c_ref, s1_ref, s2_ref, am_ref, cm_ref, s1m_ref, s2m_ref,
         o_ref, om_ref, wb_ref) = refs
    else:
        a_ref, w_ref, c_ref, s1_ref, s2_ref, o_ref, wb_ref = refs

    @pl.when(pl.program_id(1) == 0)
    def _():
        wb_ref[...] = w_ref[...].astype(wb_ref.dtype)
        if with_meta:
            _proj_rows(am_ref, wb_ref[...], cm_ref, s1m_ref, s2m_ref, om_ref, half)

    _proj_rows(a_ref, wb_ref[...], c_ref, s1_ref, s2_ref, o_ref, half)


def _proj(a, w, layer, tables, *, tm, tn, rope_cols, q_cols, half, a_meta=None, tables_meta=None):
    m, n = a.shape[0], w.shape[2]
    tab_blocks = (tables[0].shape[0] - tm) // (2 * tm)
    n_q_tiles, n_rope_tiles = q_cols // tn, rope_cols // tn
    with_meta = a_meta is not None

    def kind(j):
        return jnp.where(j < n_q_tiles, 0, jnp.where(j < n_rope_tiles, 1, 2))

    tab_spec = pl.BlockSpec((tm, LANES), lambda j, i: (kind(j) * tab_blocks + jnp.where(
        kind(j) == 2, 0, i % tab_blocks), 0))
    in_specs = [pl.BlockSpec((tm, D_MODEL), lambda j, i: (i, 0)),
                pl.BlockSpec((None, D_MODEL, tn), lambda j, i: (layer, 0, j)),
                tab_spec, tab_spec, tab_spec]
    out_specs = [pl.BlockSpec((tm, tn), lambda j, i: (i, j))]
    out_shape = [jax.ShapeDtypeStruct((m, n), BF16)]
    operands = [a, w, *tables]
    if with_meta:
        mm = a_meta.shape[0]
        tabm_spec = pl.BlockSpec((mm, LANES), lambda j, i: (kind(j), 0))
        in_specs += [pl.BlockSpec((mm, D_MODEL), lambda j, i: (0, 0)), tabm_spec, tabm_spec, tabm_spec]
        out_specs.append(pl.BlockSpec((mm, tn), lambda j, i: (0, j)))
        out_shape.append(jax.ShapeDtypeStruct((mm, n), BF16))
        operands += [a_meta, *tables_meta]
    outs = pl.pallas_call(
        functools.partial(_proj_kernel, half=half, with_meta=with_meta),
        grid=(n // tn, m // tm),
        in_specs=in_specs,
        out_specs=out_specs,
        out_shape=out_shape,
        scratch_shapes=[pltpu.VMEM((D_MODEL, tn), BF16)],
        compiler_params=_params("arbitrary", "arbitrary"),
        name="proj_rope",
    )(*operands)
    return outs if with_meta else outs[0]


def _proj_tables(pos, head_dim, half, q_scale, tm):
    ident = (jnp.ones((tm, LANES), F32), jnp.zeros((tm, LANES), F32), jnp.zeros((tm, LANES), F32))
    return tuple(jnp.concatenate([t * q_scale, t, e], axis=0)
                 for t, e in zip(_rope_tables(pos, head_dim, half), ident))


def _rope_tables(pos, head_dim, half):
    inv = ROPE_THETA ** (-jnp.arange(0, 2 * half, 2, dtype=F32) / (2 * half))
    ang = pos.astype(F32)[:, None] * inv[None, :]
    cos, sin = jnp.cos(ang), jnp.sin(ang)
    p = pos.shape[0]
    ones = jnp.ones((p, head_dim - 2 * half), F32)
    zeros = jnp.zeros((p, head_dim - 2 * half), F32)
    zh = jnp.zeros((p, half), F32)
    c = jnp.concatenate([cos, cos, ones], axis=1)
    s1 = jnp.concatenate([-sin, zh, zeros], axis=1)
    s2 = jnp.concatenate([zh, sin, zeros], axis=1)
    rep = LANES // head_dim
    return tuple(jnp.tile(t, (1, rep)) for t in (c, s1, s2))


def _lambda_value(lam_ref, lambda_init):
    lv = lam_ref[...]
    return (jnp.exp(jnp.sum(lv[0:1] * lv[1:2], axis=-1, keepdims=True))
            - jnp.exp(jnp.sum(lv[2:3] * lv[3:4], axis=-1, keepdims=True)) + lambda_init)


def _diff_epilogue(acc, l1, l2, lam, g, gate, lambda_init):
    tq = acc.shape[0] // 2
    o = acc[:tq] * (1.0 / l1) - acc[tq:] * (lam / l2)
    r = lax.rsqrt(jnp.mean(o * o, axis=-1, keepdims=True) + NORM_EPS)
    o = o * r * (g * (1.0 - lambda_init))
    gate = gate.astype(F32)
    return o * (gate * jax.nn.sigmoid(gate))


def _lane_block_reduce(x, op):
    r = x[:, :LANES]
    for c in range(1, x.shape[1] // LANES):
        r = op(r, x[:, c * LANES:(c + 1) * LANES])
    return r


def _attn_a_kernel(lam_ref, dmat_ref, qa_ref, qb_ref, k_ref, v_ref, km_ref, vm_ref, ga_ref, gb_ref,
                   g_ref, o_ref, q_sc, s_ref, mr_ref, mb_ref, ls_ref, acc_ref, *, tq, n_tiles,
                   lambda_init):
    i = pl.program_id(2)
    hd = A_HEAD_DIM
    n_steps = n_tiles + 1
    hq = tq // DIAG_PIECES
    neg_inf = jnp.float32(-jnp.inf)
    q_sc[0] = qa_ref[...]
    q_sc[1] = qb_ref[...]
    km = km_ref[...]
    vm = vm_ref[...]
    lam = _lambda_value(lam_ref, lambda_init)
    vis = dmat_ref[...] <= 0

    def step_info(t):
        if t == 0:
            return 0, i, True
        if t == n_steps - 1:
            return 1, n_tiles - 1 - i, True
        if t > n_tiles // 2 - 1:
            return 1, t - i - 1, False
        low = t <= i
        return jnp.where(low, 0, 1), jnp.where(low, i - t, t - i - 1), False

    def parts(diag):
        if not diag:
            return ((slice(0, tq), tq),)
        return tuple((slice(r * hq, (r + 1) * hq), (r + 1) * hq) for r in range(DIAG_PIECES))

    init = jnp.full((tq, LANES), neg_inf, F32)
    for h in range(2):
        for sel in range(2):
            mr_ref[h, sel] = init
            ls_ref[h, sel] = jnp.zeros((tq, LANES), F32)
    acc_ref[...] = jnp.zeros(acc_ref.shape, F32)
    for t in range(n_steps):
        sel, kt, diag = step_info(t)
        krows = k_ref[pl.ds(pl.multiple_of(kt * tq, tq), tq), :]
        for h in range(2):
            for rows, nk in parts(diag):
                s = lax.dot_general(q_sc[sel, rows, h * hd:(h + 1) * hd],
                                    krows[:nk, h * hd:(h + 1) * hd], _NT,
                                    preferred_element_type=F32)
                if diag:
                    masked = jnp.where(vis, s[:, nk - hq:], neg_inf)
                    s = masked if nk == hq else jnp.concatenate([s[:, :nk - hq], masked], axis=1)
                s_ref[h, t, rows, :nk] = s
                mr_ref[h, sel, rows, :] = jnp.maximum(mr_ref[h, sel, rows, :],
                                                      _lane_block_reduce(s, jnp.maximum))

    lane = lax.broadcasted_iota(jnp.int32, km.shape, 1)
    head0 = jnp.where(lane < hd, 1.0, 0.0).astype(BF16)
    head1 = jnp.where(lane < hd, 0.0, 1.0).astype(BF16)
    km_bd = jnp.concatenate([km * head0, km * head1], axis=0)
    zero_v = jnp.zeros_like(vm)
    vm_bd = jnp.concatenate([jnp.concatenate([vm, zero_v], axis=1),
                             jnp.concatenate([zero_v, vm], axis=1)], axis=0)
    first = lax.broadcasted_iota(jnp.int32, (1, 2 * N_META), 1) < N_META
    sms = [lax.dot_general(q_sc[sel], km_bd, _NT, preferred_element_type=F32) for sel in range(2)]

    for sel in range(2):
        for h in range(2):
            own = first if h == 0 else jnp.logical_not(first)
            m = jnp.maximum(jnp.max(jnp.where(own, sms[sel], neg_inf), axis=-1, keepdims=True),
                            jnp.max(mr_ref[h, sel], axis=-1, keepdims=True))
            mb_ref[h, sel] = jnp.broadcast_to(m, (tq, LANES))

    for t in range(n_steps):
        sel, kt, diag = step_info(t)
        vrows = v_ref[pl.ds(pl.multiple_of(kt * tq, tq), tq), :]
        for rows, nk in parts(diag):
            ps = []
            for h in range(2):
                mb = mb_ref[h, sel, rows, :]
                blocks = []
                ls = None
                for c in range(nk // LANES):
                    e = jnp.exp2(s_ref[h, t, rows, c * LANES:(c + 1) * LANES] - mb)
                    ls = e if ls is None else ls + e
                    blocks.append(e.astype(BF16))
                ls_ref[h, sel, rows, :] = ls_ref[h, sel, rows, :] + ls
                ps.append(jnp.concatenate(blocks, axis=1))
            pv = jnp.dot(jnp.concatenate(ps, axis=0), vrows[:nk], preferred_element_type=F32)
            n = rows.stop - rows.start
            for h in range(2):
                arows = slice(h * tq + rows.start, h * tq + rows.stop)
                acc_ref[sel, arows, :] = acc_ref[sel, arows, :] + pv[h * n:(h + 1) * n]

    for sel, gate_ref in ((0, ga_ref), (1, gb_ref)):
        em = jnp.exp2(sms[sel] - jnp.where(first, mb_ref[0, sel, :, :1], mb_ref[1, sel, :, :1]))
        ls = [jnp.sum(ls_ref[0, sel], axis=-1, keepdims=True)
              + jnp.sum(jnp.where(first, em, 0.0), axis=-1, keepdims=True),
              jnp.sum(ls_ref[1, sel], axis=-1, keepdims=True)
              + jnp.sum(jnp.where(first, 0.0, em), axis=-1, keepdims=True)]
        pv = jnp.dot(em.astype(BF16), vm_bd, preferred_element_type=F32)
        acc = acc_ref[sel] + jnp.concatenate([pv[:, :2 * hd], pv[:, 2 * hd:]], axis=0)
        o = _diff_epilogue(acc, ls[0], ls[1], lam, g_ref[...], gate_ref[...], lambda_init)
        o_ref[sel] = o.astype(o_ref.dtype)


def _attn_a(qkvg, qkvg_meta, lam_vecs, subln_g, *, batch, seq, tq, lambda_init):
    n_tiles = seq // tq
    half = n_tiles // 2
    hw = 2 * A_HEAD_DIM
    nh = A_HEADS
    piece = tq // DIAG_PIECES
    idx = jnp.arange(piece, dtype=jnp.int32) >> CHUNK_SHIFT
    dmat = idx[None, :] - idx[:, None]
    kern = functools.partial(_attn_a_kernel, tq=tq, n_tiles=n_tiles, lambda_init=lambda_init)
    lo = lambda b, h, i: b * n_tiles + i
    hi = lambda b, h, i: b * n_tiles + n_tiles - 1 - i
    return pl.pallas_call(
        kern,
        grid=(batch, nh, half),
        in_specs=[
            pl.BlockSpec((4, A_HEAD_DIM), lambda b, h, i: (0, 0)),
            pl.BlockSpec((piece, piece), lambda b, h, i: (0, 0)),
            pl.BlockSpec((tq, hw), lambda b, h, i: (lo(b, h, i), h)),
            pl.BlockSpec((tq, hw), lambda b, h, i: (hi(b, h, i), h)),
            pl.BlockSpec((seq, hw), lambda b, h, i: (b, nh + h)),
            pl.BlockSpec((seq, hw), lambda b, h, i: (b, 2 * nh + h)),
            pl.BlockSpec((N_META, hw), lambda b, h, i: (0, nh + h)),
            pl.BlockSpec((N_META, hw), lambda b, h, i: (0, 2 * nh + h)),
            pl.BlockSpec((tq, hw), lambda b, h, i: (lo(b, h, i), 3 * nh + h)),
            pl.BlockSpec((tq, hw), lambda b, h, i: (hi(b, h, i), 3 * nh + h)),
            pl.BlockSpec((1, hw), lambda b, h, i: (0, 0)),
        ],
        out_specs=pl.BlockSpec((2, None, None, tq, hw), lambda b, h, i: (0, b, i, 0, h)),
        out_shape=jax.ShapeDtypeStruct((2, batch, half, tq, D_MODEL), BF16),
        scratch_shapes=[pltpu.VMEM((2, tq, hw), BF16),
                        pltpu.VMEM((2, n_tiles + 1, tq, tq), F32),
                        pltpu.VMEM((2, 2, tq, LANES), F32),
                        pltpu.VMEM((2, 2, tq, LANES), F32),
                        pltpu.VMEM((2, 2, tq, LANES), F32),
                        pltpu.VMEM((2, 2 * tq, hw), F32)],
        compiler_params=_params("parallel", "parallel", "arbitrary"),
        name="diff_attn",
    )(lam_vecs, dmat, qkvg, qkvg, qkvg, qkvg, qkvg_meta, qkvg_meta, qkvg, qkvg,
      subln_g.reshape(1, hw))


def _attn_a_row_block(r, *, batch, seq, tq, tm):
    n_tiles = seq // tq
    half = n_tiles // 2
    sub = tq // tm
    b = r // (n_tiles * sub)
    u = r % (n_tiles * sub)
    tile, within = u // sub, u % sub
    hi = tile // half
    pos = jnp.where(hi == 0, tile, n_tiles - 1 - tile)
    return ((hi * batch + b) * half + pos) * sub + within


def _attn_a_meta_kernel(lam_ref, q_ref, km_ref, vm_ref, gate_ref, g_ref, o_ref, *, lambda_init):
    hd = A_HEAD_DIM
    q = q_ref[...]
    km = km_ref[...]
    ems, ls = [], []
    for h in range(2):
        sm = lax.dot_general(q[:, h * hd:(h + 1) * hd], km[:, h * hd:(h + 1) * hd], _NT,
                             preferred_element_type=F32)
        em = jnp.exp2(sm - jnp.max(sm, axis=-1, keepdims=True))
        ls.append(jnp.sum(em, axis=-1, keepdims=True))
        ems.append(em.astype(BF16))
    acc = jnp.dot(jnp.concatenate(ems, axis=0), vm_ref[...], preferred_element_type=F32)
    lam = _lambda_value(lam_ref, lambda_init)
    o = _diff_epilogue(acc, ls[0], ls[1], lam, g_ref[...], gate_ref[...], lambda_init)
    o_ref[...] = o.astype(o_ref.dtype)


def _attn_a_meta(qkvg_meta, lam_vecs, subln_g, *, lambda_init):
    hw = 2 * A_HEAD_DIM
    nh = A_HEADS
    kern = functools.partial(_attn_a_meta_kernel, lambda_init=lambda_init)
    return pl.pallas_call(
        kern,
        grid=(nh,),
        in_specs=[
            pl.BlockSpec((4, A_HEAD_DIM), lambda h: (0, 0)),
            pl.BlockSpec((META_ROWS, hw), lambda h: (0, h)),
            pl.BlockSpec((N_META, hw), lambda h: (0, nh + h)),
            pl.BlockSpec((N_META, hw), lambda h: (0, 2 * nh + h)),
            pl.BlockSpec((META_ROWS, hw), lambda h: (0, 3 * nh + h)),
            pl.BlockSpec((1, hw), lambda h: (0, 0)),
        ],
        out_specs=pl.BlockSpec((META_ROWS, hw), lambda h: (0, h)),
        out_shape=jax.ShapeDtypeStruct((META_ROWS, D_MODEL), BF16),
        compiler_params=_params("parallel"),
        name="diff_attn_meta",
    )(lam_vecs, qkvg_meta, qkvg_meta, qkvg_meta, qkvg_meta, subln_g.reshape(1, hw))


SWA_KEYS = 256
SWA_SINK_COL = N_META


def _swa_kernel(sink_ref, q_ref, kd_ref, vd_ref, kdm_ref, vdm_ref, gate_ref, o_ref, *, n_chunks):
    step = pl.program_id(1)
    tq = CHUNK
    win = 3 * CHUNK
    nk = SWA_KEYS
    neg_inf = jnp.float32(-jnp.inf)

    colh = lax.broadcasted_iota(jnp.int32, (1, 2 * nk), 1) & (nk - 1)
    lane = lax.broadcasted_iota(jnp.int32, (nk, LANES), 1)
    lo_mask = jnp.where(lane < B_HEAD_DIM, 1.0, 0.0).astype(BF16)
    hi_mask = jnp.where(lane < B_HEAD_DIM, 0.0, 1.0).astype(BF16)
    pad = jnp.zeros((CHUNK - N_META, LANES), BF16)
    indicator = jnp.concatenate([lo_mask, hi_mask], axis=0)
    sink_rows = sink_ref[...] * LOG2E

    def block_diag(meta_rows, win_rows):
        w = jnp.concatenate([meta_rows, pad, win_rows], axis=0)
        return jnp.concatenate([w * lo_mask, w * hi_mask], axis=0)

    def chunk_body(cc, carry):
        c = step * n_chunks + cc
        r0 = pl.multiple_of(cc * CHUNK, CHUNK)
        start = pl.multiple_of(jnp.maximum(c * CHUNK - 2 * CHUNK, 0), CHUNK)
        n_vis = (jnp.minimum(c, 2) + 1) * CHUNK
        valid = jnp.logical_or(colh <= SWA_SINK_COL,
                               jnp.logical_and(colh >= CHUNK, colh < CHUNK + n_vis))
        for kv in range(B_KV_HEADS):
            cs = slice(kv * LANES, (kv + 1) * LANES)
            kbd = block_diag(kdm_ref[:, cs], kd_ref[pl.ds(start, win), cs])
            vbd = jnp.concatenate([block_diag(vdm_ref[:, cs], vd_ref[pl.ds(start, win), cs]),
                                   indicator], axis=1)
            q4 = jnp.concatenate(
                [q_ref[pl.ds(r0, tq), kv * 512 + p * LANES: kv * 512 + (p + 1) * LANES]
                 for p in range(4)], axis=0)
            s = lax.dot_general(q4, kbd, _NT, preferred_element_type=F32)
            p_rows = []
            for p in range(4):
                bias = jnp.where(valid, sink_rows[kv * 4 + p:kv * 4 + p + 1, :], neg_inf)
                sp = s[p * tq:(p + 1) * tq] + bias
                p_cols = []
                for hp in range(2):
                    sq = sp[:, hp * nk:(hp + 1) * nk]
                    e = jnp.exp2(sq - jnp.max(sq, axis=-1, keepdims=True))
                    p_cols.append(e.astype(BF16))
                p_rows.append(jnp.concatenate(p_cols, axis=1))
            pmat = jnp.concatenate(p_rows, axis=0)
            o4 = jnp.dot(pmat, vbd, preferred_element_type=F32)
            on = o4[:, :LANES] * (1.0 / o4[:, LANES:])
            for p in range(4):
                osl = slice(kv * 512 + p * LANES, kv * 512 + (p + 1) * LANES)
                gate = gate_ref[pl.ds(r0, tq), osl].astype(F32)
                o_ref[pl.ds(r0, tq), osl] = (on[p * tq:(p + 1) * tq]
                                             * (gate * jax.nn.sigmoid(gate))).astype(o_ref.dtype)
        return carry

    lax.fori_loop(0, n_chunks, chunk_body, 0, unroll=4)


def _swa(qg, kvd, kvd_meta, sinks, *, batch, seq, n_chunks):
    ns = seq // (CHUNK * n_chunks)
    rows = CHUNK * n_chunks
    kw = B_KV_HEADS * LANES
    sink_rows = jnp.zeros((B_KV_HEADS * 4, 2, SWA_KEYS), F32).at[:, :, SWA_SINK_COL].set(
        sinks.astype(F32).reshape(B_KV_HEADS * 4, 2)).reshape(B_KV_HEADS * 4, 2 * SWA_KEYS)
    return pl.pallas_call(
        functools.partial(_swa_kernel, n_chunks=n_chunks),
        grid=(batch, ns),
        in_specs=[
            pl.BlockSpec((B_KV_HEADS * 4, 2 * SWA_KEYS), lambda b, c: (0, 0)),
            pl.BlockSpec((rows, D_MODEL), lambda b, c: (b * ns + c, 0)),
            pl.BlockSpec((seq, kw), lambda b, c: (b, 0)),
            pl.BlockSpec((seq, kw), lambda b, c: (b, 1)),
            pl.BlockSpec((N_META, kw), lambda b, c: (0, 0)),
            pl.BlockSpec((N_META, kw), lambda b, c: (0, 1)),
            pl.BlockSpec((rows, D_MODEL), lambda b, c: (b * ns + c, 1)),
        ],
        out_specs=pl.BlockSpec((rows, D_MODEL), lambda b, c: (b * ns + c, 0)),
        out_shape=jax.ShapeDtypeStruct((batch * seq, D_MODEL), BF16),
        compiler_params=_params("parallel", "arbitrary"),
        name="swa_attn",
    )(sink_rows, qg, kvd, kvd, kvd_meta, kvd_meta, qg)


def _out_kernel(*refs, n_norm, emit_h):
    o_ref, w_ref, h_ref = refs[:3]
    g_refs = refs[3:3 + n_norm]
    outs = refs[3 + n_norm:]
    tm = o_ref.shape[0]
    sub = min(tm, PROJ_SUB_ROWS)
    w = w_ref[...]
    for rb in range(tm // sub):
        rows = slice(rb * sub, (rb + 1) * sub)
        acc = jnp.dot(o_ref[rows, :], w, preferred_element_type=F32) + h_ref[rows, :]
        norm_outs = outs
        if emit_h:
            outs[0][rows, :] = acc
            norm_outs = outs[1:]
        r = lax.rsqrt(jnp.mean(acc * acc, axis=-1, keepdims=True) + NORM_EPS)
        xn = acc * r
        for g_ref, n_ref in zip(g_refs, norm_outs):
            n_ref[rows, :] = (xn * g_ref[...]).astype(n_ref.dtype)


def _out_proj(o, w, h, gains, *, tm, emit_h, norm_dtype, o_row_block=None):
    m = h.shape[0]
    row = pl.BlockSpec((tm, D_MODEL), lambda i: (i, 0))
    o_spec = row if o_row_block is None else pl.BlockSpec((tm, D_MODEL), lambda i: (o_row_block(i), 0))
    gspec = pl.BlockSpec((1, D_MODEL), lambda i: (0, 0))
    out_shape = ([jax.ShapeDtypeStruct((m, D_MODEL), F32)] if emit_h else []) + [
        jax.ShapeDtypeStruct((m, D_MODEL), norm_dtype) for _ in gains]
    kern = functools.partial(_out_kernel, n_norm=len(gains), emit_h=emit_h)
    return pl.pallas_call(
        kern,
        grid=(m // tm,),
        in_specs=[o_spec, pl.BlockSpec((D_MODEL, D_MODEL), lambda i: (0, 0)), row] + [gspec] * len(gains),
        out_specs=[row] * len(out_shape),
        out_shape=out_shape,
        compiler_params=_params("parallel"),
        name="out_proj",
    )(o, w, h, *[g.reshape(1, D_MODEL) for g in gains])


def kernel(x, meta_tokens, a_norm, a_w_in, a_w_out, a_lambda_q1, a_lambda_k1, a_lambda_q2,
           a_lambda_k2, a_subln, kv_norm, w_kv, b_norm, b_w_in, b_w_out, b_sinks, final_norm):
    batch, seq, d = x.shape
    n_a = a_w_in.shape[0]
    n_b = b_w_in.shape[0]
    h_r = x.reshape(batch * seq, d)
    h_m = jnp.concatenate([meta_tokens.astype(x.dtype),
                           jnp.zeros((META_ROWS - N_META, d), x.dtype)], axis=0)

    tm_r = 1024
    tm_out = 512
    tq_a = 512
    a_scale = A_HEAD_DIM ** -0.5 * LOG2E
    b_scale = B_HEAD_DIM ** -0.5 * LOG2E
    attn_rows = functools.partial(_attn_a_row_block, batch=batch, seq=seq, tq=tq_a, tm=tm_out)

    pos_r = jnp.arange(seq, dtype=jnp.int32) + N_META
    pos_m = jnp.arange(META_ROWS, dtype=jnp.int32)
    tab_a_r = _proj_tables(pos_r, A_HEAD_DIM, A_ROT_HALF, a_scale, tm_r)
    tab_a_m = _proj_tables(pos_m, A_HEAD_DIM, A_ROT_HALF, a_scale, META_ROWS)
    tab_b_r = _proj_tables(pos_r, B_HEAD_DIM, B_ROT_HALF, b_scale, tm_r)
    tab_b_m = _proj_tables(pos_m, B_HEAD_DIM, B_ROT_HALF, b_scale, META_ROWS)

    wk, wv = jnp.split(w_kv, 2, axis=1)

    def dup_heads(w):
        w = w.reshape(d, B_KV_HEADS, 1, B_HEAD_DIM)
        return jnp.broadcast_to(w, (d, B_KV_HEADS, 2, B_HEAD_DIM)).reshape(d, B_KV_HEADS * LANES)

    w_kvd = jnp.concatenate([dup_heads(wk), dup_heads(wv)], axis=1)

    hn_r = _rms_norm(h_r, a_norm[0], 512)
    hn_m = _rms_norm(h_m, a_norm[0], META_ROWS)
    for l in range(n_a):
        lambda_init = 0.8 - 0.6 * math.exp(-0.3 * l)
        proj_kw = dict(tn=1024, rope_cols=2 * d, q_cols=d, half=A_ROT_HALF)
        w_out = a_w_out[l].astype(BF16)
        qkvg_r, qkvg_m = _proj(hn_r, a_w_in, l, tab_a_r, tm=tm_r, a_meta=hn_m, tables_meta=tab_a_m,
                                **proj_kw)
        lam_vecs = jnp.stack([a_lambda_q1[l], a_lambda_k1[l], a_lambda_q2[l], a_lambda_k2[l]]).astype(F32)
        o_r = _attn_a(qkvg_r, qkvg_m, lam_vecs, a_subln[l], batch=batch, seq=seq, tq=tq_a,
                      lambda_init=lambda_init)
        o_m = _attn_a_meta(qkvg_m, lam_vecs, a_subln[l], lambda_init=lambda_init)
        if l + 1 < n_a:
            gains_r = gains_m = [a_norm[l + 1]]
        else:
            gains_r, gains_m = [kv_norm, b_norm[0]], [kv_norm]
        outs_r = _out_proj(o_r.reshape(batch * seq, d), w_out, h_r, gains_r, tm=tm_out,
                           emit_h=True, norm_dtype=BF16, o_row_block=attn_rows)
        outs_m = _out_proj(o_m, w_out, h_m, gains_m, tm=META_ROWS, emit_h=True, norm_dtype=BF16)
        h_r, h_m = outs_r[0], outs_m[0]
        hn_r, hn_m = outs_r[1], outs_m[1]

    kv_kw = dict(tn=512, rope_cols=B_KV_HEADS * LANES, q_cols=0, half=B_ROT_HALF)
    kvd_r, kvd_m = _proj(hn_r, w_kvd[None], 0, tab_b_r, tm=tm_r, a_meta=hn_m, tables_meta=tab_b_m,
                         **kv_kw)
    hn_r = outs_r[2]

    for l in range(n_b):
        qg = _proj(hn_r, b_w_in, l, tab_b_r, tm=tm_r, tn=1024, rope_cols=d, q_cols=d,
                   half=B_ROT_HALF)
        o_r = _swa(qg, kvd_r, kvd_m, b_sinks[l], batch=batch, seq=seq, n_chunks=8)
        w_out = b_w_out[l].astype(BF16)
        if l + 1 < n_b:
            h_r, hn_r = _out_proj(o_r, w_out, h_r, [b_norm[l + 1]], tm=tm_out, emit_h=True,
                                  norm_dtype=BF16)
        else:
            (out,) = _out_proj(o_r, w_out, h_r, [final_norm], tm=tm_out, emit_h=False,
                               norm_dtype=x.dtype)
    return out.reshape(batch, seq, d)
```

```python
import functools
import math

import jax
import jax.numpy as jnp
from jax import lax
from jax.experimental import pallas as pl
from jax.experimental.pallas import tpu as pltpu

D_MODEL = 2048
N_META = 16
META_ROWS = 64
CHUNK = 64
CHUNK_SHIFT = 6
ROPE_THETA = 500000.0
NORM_EPS = 1e-5
A_HEADS = 8
A_HEAD_DIM = 128
A_ROT_HALF = 16
B_HEAD_DIM = 64
B_KV_HEADS = 4
B_ROT_HALF = 8
LANES = 128
VMEM_LIMIT = 56 * 1024 * 1024
LOG2E = 1.4426950408889634

F32 = jnp.float32
BF16 = jnp.bfloat16
_NT = (((1,), (1,)), ((), ()))


def _params(*sem):
    return pltpu.CompilerParams(dimension_semantics=sem, vmem_limit_bytes=VMEM_LIMIT)


def _norm_kernel(x_ref, g_ref, o_ref):
    x = x_ref[...]
    r = lax.rsqrt(jnp.mean(x * x, axis=-1, keepdims=True) + NORM_EPS)
    o_ref[...] = (x * r * g_ref[...]).astype(o_ref.dtype)


def _rms_norm(x, g, tm):
    m = x.shape[0]
    return pl.pallas_call(
        _norm_kernel,
        grid=(m // tm,),
        in_specs=[pl.BlockSpec((tm, D_MODEL), lambda i: (i, 0)),
                  pl.BlockSpec((1, D_MODEL), lambda i: (0, 0))],
        out_specs=pl.BlockSpec((tm, D_MODEL), lambda i: (i, 0)),
        out_shape=jax.ShapeDtypeStruct((m, D_MODEL), BF16),
        compiler_params=_params("parallel"),
        name="rms_norm",
    )(x, g.reshape(1, D_MODEL))


PROJ_SUB_ROWS = 128
DIAG_PIECES = 4


def _proj_rows(a_ref, w, c_ref, s1_ref, s2_ref, o_ref, half):
    tm, tn = o_ref.shape
    sub = min(tm, PROJ_SUB_ROWS)
    for rb in range(tm // sub):
        rows = slice(rb * sub, (rb + 1) * sub)
        acc = jnp.dot(a_ref[rows, :], w, preferred_element_type=F32)
        c, s1, s2 = c_ref[rows, :], s1_ref[rows, :], s2_ref[rows, :]
        for cb in range(tn // LANES):
            x = acc[:, cb * LANES:(cb + 1) * LANES]
            y = (x * c + pltpu.roll(x, LANES - half, 1) * s1 + pltpu.roll(x, half, 1) * s2)
            o_ref[rows, cb * LANES:(cb + 1) * LANES] = y.astype(o_ref.dtype)


def _proj_kernel(*refs, half, with_meta):
    if with_meta:
        (a_ref, w_ref, c_ref, s1_ref, s2_ref, am_ref, cm_ref, s1m_ref, s2m_ref,
         o_ref, om_ref, wb_ref) = refs
    else:
        a_ref, w_ref, c_ref, s1_ref, s2_ref, o_ref, wb_ref = refs

    @pl.when(pl.program_id(1) == 0)
    def _():
        wb_ref[...] = w_ref[...].astype(wb_ref.dtype)
        if with_meta:
            _proj_rows(am_ref, wb_ref[...], cm_ref, s1m_ref, s2m_ref, om_ref, half)

    _proj_rows(a_ref, wb_ref[...], c_ref, s1_ref, s2_ref, o_ref, half)


def _proj(a, w, layer, tables, *, tm, tn, rope_cols, q_cols, half, a_meta=None, tables_meta=None):
    m, n = a.shape[0], w.shape[2]
    tab_blocks = (tables[0].shape[0] - tm) // (2 * tm)
    n_q_tiles, n_rope_tiles = q_cols // tn, rope_cols // tn
    with_meta = a_meta is not None

    def kind(j):
        return jnp.where(j < n_q_tiles, 0, jnp.where(j < n_rope_tiles, 1, 2))

    tab_spec = pl.BlockSpec((tm, LANES), lambda j, i: (kind(j) * tab_blocks + jnp.where(
        kind(j) == 2, 0, i % tab_blocks), 0))
    in_specs = [pl.BlockSpec((tm, D_MODEL), lambda j, i: (i, 0)),
                pl.BlockSpec((None, D_MODEL, tn), lambda j, i: (layer, 0, j),
                             pipeline_mode=pl.Buffered(1)),
                tab_spec, tab_spec, tab_spec]
    out_specs = [pl.BlockSpec((tm, tn), lambda j, i: (i, j))]
    out_shape = [jax.ShapeDtypeStruct((m, n), BF16)]
    operands = [a, w, *tables]
    if with_meta:
        mm = a_meta.shape[0]
        tabm_spec = pl.BlockSpec((mm, LANES), lambda j, i: (kind(j), 0))
        in_specs += [pl.BlockSpec((mm, D_MODEL), lambda j, i: (0, 0)), tabm_spec, tabm_spec, tabm_spec]
        out_specs.append(pl.BlockSpec((mm, tn), lambda j, i: (0, j)))
        out_shape.append(jax.ShapeDtypeStruct((mm, n), BF16))
        operands += [a_meta, *tables_meta]
    outs = pl.pallas_call(
        functools.partial(_proj_kernel, half=half, with_meta=with_meta),
        grid=(n // tn, m // tm),
        in_specs=in_specs,
        out_specs=out_specs,
        out_shape=out_shape,
        scratch_shapes=[pltpu.VMEM((D_MODEL, tn), BF16)],
        compiler_params=_params("arbitrary", "arbitrary"),
        name="proj_rope",
    )(*operands)
    return outs if with_meta else outs[0]


def _proj_tables(pos, head_dim, half, q_scale, tm):
    ident = (jnp.ones((tm, LANES), F32), jnp.zeros((tm, LANES), F32), jnp.zeros((tm, LANES), F32))
    return tuple(jnp.concatenate([t * q_scale, t, e], axis=0)
                 for t, e in zip(_rope_tables(pos, head_dim, half), ident))


def _rope_tables(pos, head_dim, half):
    inv = ROPE_THETA ** (-jnp.arange(0, 2 * half, 2, dtype=F32) / (2 * half))
    ang = pos.astype(F32)[:, None] * inv[None, :]
    cos, sin = jnp.cos(ang), jnp.sin(ang)
    p = pos.shape[0]
    ones = jnp.ones((p, head_dim - 2 * half), F32)
    zeros = jnp.zeros((p, head_dim - 2 * half), F32)
    zh = jnp.zeros((p, half), F32)
    c = jnp.concatenate([cos, cos, ones], axis=1)
    s1 = jnp.concatenate([-sin, zh, zeros], axis=1)
    s2 = jnp.concatenate([zh, sin, zeros], axis=1)
    rep = LANES // head_dim
    return tuple(jnp.tile(t, (1, rep)) for t in (c, s1, s2))


def _lambda_value(lam_ref, lambda_init):
    lv = lam_ref[...]
    return (jnp.exp(jnp.sum(lv[0:1] * lv[1:2], axis=-1, keepdims=True))
            - jnp.exp(jnp.sum(lv[2:3] * lv[3:4], axis=-1, keepdims=True)) + lambda_init)


def _diff_epilogue(acc, l1, l2, lam, g, gate, lambda_init):
    tq = acc.shape[0] // 2
    o = acc[:tq] * (1.0 / l1) - acc[tq:] * (lam / l2)
    r = lax.rsqrt(jnp.mean(o * o, axis=-1, keepdims=True) + NORM_EPS)
    o = o * r * (g * (1.0 - lambda_init))
    gate = gate.astype(F32)
    return o * (gate * jax.nn.sigmoid(gate))


def _lane_block_reduce(x, op):
    r = x[:, :LANES]
    for c in range(1, x.shape[1] // LANES):
        r = op(r, x[:, c * LANES:(c + 1) * LANES])
    return r


def _attn_a_kernel(lam_ref, dmat_ref, qa_ref, qb_ref, k_ref, v_ref, km_ref, vm_ref, ga_ref, gb_ref,
                   g_ref, o_ref, q_sc, s_ref, mr_ref, mb_ref, ls_ref, acc_ref, *, tq, n_tiles,
                   lambda_init):
    i = pl.program_id(2)
    hd = A_HEAD_DIM
    n_steps = n_tiles + 1
    hq = tq // DIAG_PIECES
    neg_inf = jnp.float32(-jnp.inf)
    q_sc[0] = qa_ref[...]
    q_sc[1] = qb_ref[...]
    km = km_ref[...]
    vm = vm_ref[...]
    lam = _lambda_value(lam_ref, lambda_init)
    vis = dmat_ref[...] <= 0

    def step_info(t):
        if t == 0:
            return 0, i, True
        if t == n_steps - 1:
            return 1, n_tiles - 1 - i, True
        if t > n_tiles // 2 - 1:
            return 1, t - i - 1, False
        low = t <= i
        return jnp.where(low, 0, 1), jnp.where(low, i - t, t - i - 1), False

    def parts(diag):
        if not diag:
            return ((slice(0, tq), tq),)
        return tuple((slice(r * hq, (r + 1) * hq), (r + 1) * hq) for r in range(DIAG_PIECES))

    init = jnp.full((tq, LANES), neg_inf, F32)
    for h in range(2):
        for sel in range(2):
            mr_ref[h, sel] = init
            ls_ref[h, sel] = jnp.zeros((tq, LANES), F32)
    acc_ref[...] = jnp.zeros(acc_ref.shape, F32)
    for t in range(n_steps):
        sel, kt, diag = step_info(t)
        krows = k_ref[pl.ds(pl.multiple_of(kt * tq, tq), tq), :]
        for h in range(2):
            for rows, nk in parts(diag):
                s = lax.dot_general(q_sc[sel, rows, h * hd:(h + 1) * hd],
                                    krows[:nk, h * hd:(h + 1) * hd], _NT,
                                    preferred_element_type=F32)
                if diag:
                    masked = jnp.where(vis, s[:, nk - hq:], neg_inf)
                    s = masked if nk == hq else jnp.concatenate([s[:, :nk - hq], masked], axis=1)
                s_ref[h, t, rows, :nk] = s
                mr_ref[h, sel, rows, :] = jnp.maximum(mr_ref[h, sel, rows, :],
                                                      _lane_block_reduce(s, jnp.maximum))

    lane = lax.broadcasted_iota(jnp.int32, km.shape, 1)
    head0 = jnp.where(lane < hd, 1.0, 0.0).astype(BF16)
    head1 = jnp.where(lane < hd, 0.0, 1.0).astype(BF16)
    km_bd = jnp.concatenate([km * head0, km * head1], axis=0)
    zero_v = jnp.zeros_like(vm)
    vm_bd = jnp.concatenate([jnp.concatenate([vm, zero_v], axis=1),
                             jnp.concatenate([zero_v, vm], axis=1)], axis=0)
    first = lax.broadcasted_iota(jnp.int32, (1, 2 * N_META), 1) < N_META
    sms = [lax.dot_general(q_sc[sel], km_bd, _NT, preferred_element_type=F32) for sel in range(2)]

    for sel in range(2):
        for h in range(2):
            own = first if h == 0 else jnp.logical_not(first)
            m = jnp.maximum(jnp.max(jnp.where(own, sms[sel], neg_inf), axis=-1, keepdims=True),
                            jnp.max(mr_ref[h, sel], axis=-1, keepdims=True))
            mb_ref[h, sel] = jnp.broadcast_to(m, (tq, LANES))

    for t in range(n_steps):
        sel, kt, diag = step_info(t)
        vrows = v_ref[pl.ds(pl.multiple_of(kt * tq, tq), tq), :]
        for rows, nk in parts(diag):
            ps = []
            for h in range(2):
                mb = mb_ref[h, sel, rows, :]
                blocks = []
                ls = None
                for c in range(nk // LANES):
                    e = jnp.exp2(s_ref[h, t, rows, c * LANES:(c + 1) * LANES] - mb)
                    ls = e if ls is None else ls + e
                    blocks.append(e.astype(BF16))
                ls_ref[h, sel, rows, :] = ls_ref[h, sel, rows, :] + ls
                ps.append(jnp.concatenate(blocks, axis=1))
            pv = jnp.dot(jnp.concatenate(ps, axis=0), vrows[:nk], preferred_element_type=F32)
            n = rows.stop - rows.start
            for h in range(2):
                arows = slice(h * tq + rows.start, h * tq + rows.stop)
                acc_ref[sel, arows, :] = acc_ref[sel, arows, :] + pv[h * n:(h + 1) * n]

    for sel, gate_ref in ((0, ga_ref), (1, gb_ref)):
        em = jnp.exp2(sms[sel] - jnp.where(first, mb_ref[0, sel, :, :1], mb_ref[1, sel, :, :1]))
        ls = [jnp.sum(ls_ref[0, sel], axis=-1, keepdims=True)
              + jnp.sum(jnp.where(first, em, 0.0), axis=-1, keepdims=True),
              jnp.sum(ls_ref[1, sel], axis=-1, keepdims=True)
              + jnp.sum(jnp.where(first, 0.0, em), axis=-1, keepdims=True)]
        pv = jnp.dot(em.astype(BF16), vm_bd, preferred_element_type=F32)
        acc = acc_ref[sel] + jnp.concatenate([pv[:, :2 * hd], pv[:, 2 * hd:]], axis=0)
        o = _diff_epilogue(acc, ls[0], ls[1], lam, g_ref[...], gate_ref[...], lambda_init)
        o_ref[sel] = o.astype(o_ref.dtype)


def _attn_a(qkvg, qkvg_meta, lam_vecs, subln_g, *, batch, seq, tq, lambda_init):
    n_tiles = seq // tq
    half = n_tiles // 2
    hw = 2 * A_HEAD_DIM
    nh = A_HEADS
    piece = tq // DIAG_PIECES
    idx = jnp.arange(piece, dtype=jnp.int32) >> CHUNK_SHIFT
    dmat = idx[None, :] - idx[:, None]
    kern = functools.partial(_attn_a_kernel, tq=tq, n_tiles=n_tiles, lambda_init=lambda_init)
    lo = lambda b, h, i: b * n_tiles + i
    hi = lambda b, h, i: b * n_tiles + n_tiles - 1 - i
    return pl.pallas_call(
        kern,
        grid=(batch, nh, half),
        in_specs=[
            pl.BlockSpec((4, A_HEAD_DIM), lambda b, h, i: (0, 0)),
            pl.BlockSpec((piece, piece), lambda b, h, i: (0, 0)),
            pl.BlockSpec((tq, hw), lambda b, h, i: (lo(b, h, i), h)),
            pl.BlockSpec((tq, hw), lambda b, h, i: (hi(b, h, i), h)),
            pl.BlockSpec((seq, hw), lambda b, h, i: (b, nh + h)),
            pl.BlockSpec((seq, hw), lambda b, h, i: (b, 2 * nh + h)),
            pl.BlockSpec((N_META, hw), lambda b, h, i: (0, nh + h)),
            pl.BlockSpec((N_META, hw), lambda b, h, i: (0, 2 * nh + h)),
            pl.BlockSpec((tq, hw), lambda b, h, i: (lo(b, h, i), 3 * nh + h)),
            pl.BlockSpec((tq, hw), lambda b, h, i: (hi(b, h, i), 3 * nh + h)),
            pl.BlockSpec((1, hw), lambda b, h, i: (0, 0)),
        ],
        out_specs=pl.BlockSpec((2, None, None, tq, hw), lambda b, h, i: (0, b, i, 0, h)),
        out_shape=jax.ShapeDtypeStruct((2, batch, half, tq, D_MODEL), BF16),
        scratch_shapes=[pltpu.VMEM((2, tq, hw), BF16),
                        pltpu.VMEM((2, n_tiles + 1, tq, tq), F32),
                        pltpu.VMEM((2, 2, tq, LANES), F32),
                        pltpu.VMEM((2, 2, tq, LANES), F32),
                        pltpu.VMEM((2, 2, tq, LANES), F32),
                        pltpu.VMEM((2, 2 * tq, hw), F32)],
        compiler_params=_params("parallel", "parallel", "arbitrary"),
        name="diff_attn",
    )(lam_vecs, dmat, qkvg, qkvg, qkvg, qkvg, qkvg_meta, qkvg_meta, qkvg, qkvg,
      subln_g.reshape(1, hw))


def _attn_a_row_block(r, *, batch, seq, tq, tm):
    n_tiles = seq // tq
    half = n_tiles // 2
    sub = tq // tm
    b = r // (n_tiles * sub)
    u = r % (n_tiles * sub)
    tile, within = u // sub, u % sub
    hi = tile // half
    pos = jnp.where(hi == 0, tile, n_tiles - 1 - tile)
    return ((hi * batch + b) * half + pos) * sub + within


def _attn_a_meta_kernel(lam_ref, q_ref, km_ref, vm_ref, gate_ref, g_ref, o_ref, *, lambda_init):
    hd = A_HEAD_DIM
    q = q_ref[...]
    km = km_ref[...]
    ems, ls = [], []
    for h in range(2):
        sm = lax.dot_general(q[:, h * hd:(h + 1) * hd], km[:, h * hd:(h + 1) * hd], _NT,
                             preferred_element_type=F32)
        em = jnp.exp2(sm - jnp.max(sm, axis=-1, keepdims=True))
        ls.append(jnp.sum(em, axis=-1, keepdims=True))
        ems.append(em.astype(BF16))
    acc = jnp.dot(jnp.concatenate(ems, axis=0), vm_ref[...], preferred_element_type=F32)
    lam = _lambda_value(lam_ref, lambda_init)
    o = _diff_epilogue(acc, ls[0], ls[1], lam, g_ref[...], gate_ref[...], lambda_init)
    o_ref[...] = o.astype(o_ref.dtype)


def _attn_a_meta(qkvg_meta, lam_vecs, subln_g, *, lambda_init):
    hw = 2 * A_HEAD_DIM
    nh = A_HEADS
    kern = functools.partial(_attn_a_meta_kernel, lambda_init=lambda_init)
    return pl.pallas_call(
        kern,
        grid=(nh,),
        in_specs=[
            pl.BlockSpec((4, A_HEAD_DIM), lambda h: (0, 0)),
            pl.BlockSpec((META_ROWS, hw), lambda h: (0, h)),
            pl.BlockSpec((N_META, hw), lambda h: (0, nh + h)),
            pl.BlockSpec((N_META, hw), lambda h: (0, 2 * nh + h)),
            pl.BlockSpec((META_ROWS, hw), lambda h: (0, 3 * nh + h)),
            pl.BlockSpec((1, hw), lambda h: (0, 0)),
        ],
        out_specs=pl.BlockSpec((META_ROWS, hw), lambda h: (0, h)),
        out_shape=jax.ShapeDtypeStruct((META_ROWS, D_MODEL), BF16),
        compiler_params=_params("parallel"),
        name="diff_attn_meta",
    )(lam_vecs, qkvg_meta, qkvg_meta, qkvg_meta, qkvg_meta, subln_g.reshape(1, hw))


SWA_KEYS = 256
SWA_SINK_COL = N_META


def _swa_kernel(sink_ref, q_ref, kd_ref, vd_ref, kdm_ref, vdm_ref, gate_ref, o_ref, *, n_chunks):
    step = pl.program_id(1)
    tq = CHUNK
    win = 3 * CHUNK
    nk = SWA_KEYS
    neg_inf = jnp.float32(-jnp.inf)

    colh = lax.broadcasted_iota(jnp.int32, (1, 2 * nk), 1) & (nk - 1)
    lane = lax.broadcasted_iota(jnp.int32, (nk, LANES), 1)
    lo_mask = jnp.where(lane < B_HEAD_DIM, 1.0, 0.0).astype(BF16)
    hi_mask = jnp.where(lane < B_HEAD_DIM, 0.0, 1.0).astype(BF16)
    pad = jnp.zeros((CHUNK - N_META, LANES), BF16)
    indicator = jnp.concatenate([lo_mask, hi_mask], axis=0)
    sink_rows = sink_ref[...] * LOG2E

    def block_diag(meta_rows, win_rows):
        w = jnp.concatenate([meta_rows, pad, win_rows], axis=0)
        return jnp.concatenate([w * lo_mask, w * hi_mask], axis=0)

    def chunk_body(cc, carry):
        c = step * n_chunks + cc
        r0 = pl.multiple_of(cc * CHUNK, CHUNK)
        start = pl.multiple_of(jnp.maximum(c * CHUNK - 2 * CHUNK, 0), CHUNK)
        n_vis = (jnp.minimum(c, 2) + 1) * CHUNK
        valid = jnp.logical_or(colh <= SWA_SINK_COL,
                               jnp.logical_and(colh >= CHUNK, colh < CHUNK + n_vis))
        for kv in range(B_KV_HEADS):
            cs = slice(kv * LANES, (kv + 1) * LANES)
            kbd = block_diag(kdm_ref[:, cs], kd_ref[pl.ds(start, win), cs])
            vbd = jnp.concatenate([block_diag(vdm_ref[:, cs], vd_ref[pl.ds(start, win), cs]),
                                   indicator], axis=1)
            q4 = jnp.concatenate(
                [q_ref[pl.ds(r0, tq), kv * 512 + p * LANES: kv * 512 + (p + 1) * LANES]
                 for p in range(4)], axis=0)
            s = lax.dot_general(q4, kbd, _NT, preferred_element_type=F32)
            p_rows = []
            for p in range(4):
                bias = jnp.where(valid, sink_rows[kv * 4 + p:kv * 4 + p + 1, :], neg_inf)
                sp = s[p * tq:(p + 1) * tq] + bias
                p_cols = []
                for hp in range(2):
                    sq = sp[:, hp * nk:(hp + 1) * nk]
                    e = jnp.exp2(sq - jnp.max(sq, axis=-1, keepdims=True))
                    p_cols.append(e.astype(BF16))
                p_rows.append(jnp.concatenate(p_cols, axis=1))
            pmat = jnp.concatenate(p_rows, axis=0)
            o4 = jnp.dot(pmat, vbd, preferred_element_type=F32)
            on = o4[:, :LANES] * (1.0 / o4[:, LANES:])
            for p in range(4):
                osl = slice(kv * 512 + p * LANES, kv * 512 + (p + 1) * LANES)
                gate = gate_ref[pl.ds(r0, tq), osl].astype(F32)
                o_ref[pl.ds(r0, tq), osl] = (on[p * tq:(p + 1) * tq]
                                             * (gate * jax.nn.sigmoid(gate))).astype(o_ref.dtype)
        return carry

    lax.fori_loop(0, n_chunks, chunk_body, 0, unroll=8)


def _swa(qg, kvd, kvd_meta, sinks, *, batch, seq, n_chunks):
    ns = seq // (CHUNK * n_chunks)
    rows = CHUNK * n_chunks
    kw = B_KV_HEADS * LANES
    sink_rows = jnp.zeros((B_KV_HEADS * 4, 2, SWA_KEYS), F32).at[:, :, SWA_SINK_COL].set(
        sinks.astype(F32).reshape(B_KV_HEADS * 4, 2)).reshape(B_KV_HEADS * 4, 2 * SWA_KEYS)
    return pl.pallas_call(
        functools.partial(_swa_kernel, n_chunks=n_chunks),
        grid=(batch, ns),
        in_specs=[
            pl.BlockSpec((B_KV_HEADS * 4, 2 * SWA_KEYS), lambda b, c: (0, 0)),
            pl.BlockSpec((rows, D_MODEL), lambda b, c: (b * ns + c, 0)),
            pl.BlockSpec((seq, kw), lambda b, c: (b, 0)),
            pl.BlockSpec((seq, kw), lambda b, c: (b, 1)),
            pl.BlockSpec((N_META, kw), lambda b, c: (0, 0)),
            pl.BlockSpec((N_META, kw), lambda b, c: (0, 1)),
            pl.BlockSpec((rows, D_MODEL), lambda b, c: (b * ns + c, 1)),
        ],
        out_specs=pl.BlockSpec((rows, D_MODEL), lambda b, c: (b * ns + c, 0)),
        out_shape=jax.ShapeDtypeStruct((batch * seq, D_MODEL), BF16),
        compiler_params=_params("parallel", "arbitrary"),
        name="swa_attn",
    )(sink_rows, qg, kvd, kvd, kvd_meta, kvd_meta, qg)


def _out_kernel(*refs, n_norm, emit_h):
    o_ref, w_ref, h_ref = refs[:3]
    g_refs = refs[3:3 + n_norm]
    outs = refs[3 + n_norm:]
    tm = o_ref.shape[0]
    sub = min(tm, PROJ_SUB_ROWS)
    w = w_ref[...]
    for rb in range(tm // sub):
        rows = slice(rb * sub, (rb + 1) * sub)
        acc = jnp.dot(o_ref[rows, :], w, preferred_element_type=F32) + h_ref[rows, :]
        norm_outs = outs
        if emit_h:
            outs[0][rows, :] = acc
            norm_outs = outs[1:]
        r = lax.rsqrt(jnp.mean(acc * acc, axis=-1, keepdims=True) + NORM_EPS)
        xn = acc * r
        for g_ref, n_ref in zip(g_refs, norm_outs):
            n_ref[rows, :] = (xn * g_ref[...]).astype(n_ref.dtype)


def _out_proj(o, w, h, gains, *, tm, emit_h, norm_dtype, o_row_block=None):
    m = h.shape[0]
    row = pl.BlockSpec((tm, D_MODEL), lambda i: (i, 0))
    o_spec = row if o_row_block is None else pl.BlockSpec((tm, D_MODEL), lambda i: (o_row_block(i), 0))
    gspec = pl.BlockSpec((1, D_MODEL), lambda i: (0, 0))
    out_shape = ([jax.ShapeDtypeStruct((m, D_MODEL), F32)] if emit_h else []) + [
        jax.ShapeDtypeStruct((m, D_MODEL), norm_dtype) for _ in gains]
    kern = functools.partial(_out_kernel, n_norm=len(gains), emit_h=emit_h)
    return pl.pallas_call(
        kern,
        grid=(m // tm,),
        in_specs=[o_spec, pl.BlockSpec((D_MODEL, D_MODEL), lambda i: (0, 0)), row] + [gspec] * len(gains),
        out_specs=[row] * len(out_shape),
        out_shape=out_shape,
        compiler_params=_params("parallel"),
        name="out_proj",
    )(o, w, h, *[g.reshape(1, D_MODEL) for g in gains])


def kernel(x, meta_tokens, a_norm, a_w_in, a_w_out, a_lambda_q1, a_lambda_k1, a_lambda_q2,
           a_lambda_k2, a_subln, kv_norm, w_kv, b_norm, b_w_in, b_w_out, b_sinks, final_norm):
    batch, seq, d = x.shape
    n_a = a_w_in.shape[0]
    n_b = b_w_in.shape[0]
    h_r = x.reshape(batch * seq, d)
    h_m = jnp.concatenate([meta_tokens.astype(x.dtype),
                           jnp.zeros((META_ROWS - N_META, d), x.dtype)], axis=0)

    tm_r = 2048
    tm_out = 512
    tq_a = 512
    a_scale = A_HEAD_DIM ** -0.5 * LOG2E
    b_scale = B_HEAD_DIM ** -0.5 * LOG2E
    attn_rows = functools.partial(_attn_a_row_block, batch=batch, seq=seq, tq=tq_a, tm=tm_out)

    pos_r = jnp.arange(seq, dtype=jnp.int32) + N_META
    pos_m = jnp.arange(META_ROWS, dtype=jnp.int32)
    tab_a_r = _proj_tables(pos_r, A_HEAD_DIM, A_ROT_HALF, a_scale, tm_r)
    tab_a_m = _proj_tables(pos_m, A_HEAD_DIM, A_ROT_HALF, a_scale, META_ROWS)
    tab_b_r = _proj_tables(pos_r, B_HEAD_DIM, B_ROT_HALF, b_scale, tm_r)
    tab_b_m = _proj_tables(pos_m, B_HEAD_DIM, B_ROT_HALF, b_scale, META_ROWS)

    wk, wv = jnp.split(w_kv, 2, axis=1)

    def dup_heads(w):
        w = w.reshape(d, B_KV_HEADS, 1, B_HEAD_DIM)
        return jnp.broadcast_to(w, (d, B_KV_HEADS, 2, B_HEAD_DIM)).reshape(d, B_KV_HEADS * LANES)

    w_kvd = jnp.concatenate([dup_heads(wk), dup_heads(wv)], axis=1)

    hn_r = _rms_norm(h_r, a_norm[0], 512)
    hn_m = _rms_norm(h_m, a_norm[0], META_ROWS)
    for l in range(n_a):
        lambda_init = 0.8 - 0.6 * math.exp(-0.3 * l)
        proj_kw = dict(tn=1024, rope_cols=2 * d, q_cols=d, half=A_ROT_HALF)
        w_out = a_w_out[l].astype(BF16)
        qkvg_r, qkvg_m = _proj(hn_r, a_w_in, l, tab_a_r, tm=tm_r, a_meta=hn_m, tables_meta=tab_a_m,
                                **proj_kw)
        lam_vecs = jnp.stack([a_lambda_q1[l], a_lambda_k1[l], a_lambda_q2[l], a_lambda_k2[l]]).astype(F32)
        o_r = _attn_a(qkvg_r, qkvg_m, lam_vecs, a_subln[l], batch=batch, seq=seq, tq=tq_a,
                      lambda_init=lambda_init)
        o_m = _attn_a_meta(qkvg_m, lam_vecs, a_subln[l], lambda_init=lambda_init)
        if l + 1 < n_a:
            gains_r = gains_m = [a_norm[l + 1]]
        else:
            gains_r, gains_m = [kv_norm, b_norm[0]], [kv_norm]
        outs_r = _out_proj(o_r.reshape(batch * seq, d), w_out, h_r, gains_r, tm=tm_out,
                           emit_h=True, norm_dtype=BF16, o_row_block=attn_rows)
        outs_m = _out_proj(o_m, w_out, h_m, gains_m, tm=META_ROWS, emit_h=True, norm_dtype=BF16)
        h_r, h_m = outs_r[0], outs_m[0]
        hn_r, hn_m = outs_r[1], outs_m[1]

    kv_kw = dict(tn=512, rope_cols=B_KV_HEADS * LANES, q_cols=0, half=B_ROT_HALF)
    kvd_r, kvd_m = _proj(hn_r, w_kvd[None], 0, tab_b_r, tm=tm_r, a_meta=hn_m, tables_meta=tab_b_m,
                         **kv_kw)
    hn_r = outs_r[2]

    for l in range(n_b):
        qg = _proj(hn_r, b_w_in, l, tab_b_r, tm=tm_r, tn=1024, rope_cols=d, q_cols=d,
                   half=B_ROT_HALF)
        o_r = _swa(qg, kvd_r, kvd_m, b_sinks[l], batch=batch, seq=seq, n_chunks=8)
        w_out = b_w_out[l].astype(BF16)
        if l + 1 < n_b:
            h_r, hn_r = _out_proj(o_r, w_out, h_r, [b_norm[l + 1]], tm=tm_out, emit_h=True,
                                  norm_dtype=BF16)
        else:
            (out,) = _out_proj(o_r, w_out, h_r, [final_norm], tm=tm_out, emit_h=False,
                               norm_dtype=x.dtype)
    return out.reshape(batch, seq, d)
```

```python
import functools
import math

import jax
import jax.numpy as jnp
from jax import lax
from jax.experimental import pallas as pl
from jax.experimental.pallas import tpu as pltpu

D_MODEL = 2048
N_META = 16
META_ROWS = 64
CHUNK = 64
CHUNK_SHIFT = 6
ROPE_THETA = 500000.0
NORM_EPS = 1e-5
A_HEADS = 8
A_HEAD_DIM = 128
A_ROT_HALF = 16
B_HEAD_DIM = 64
B_KV_HEADS = 4
B_ROT_HALF = 8
LANES = 128
VMEM_LIMIT = 56 * 1024 * 1024
LOG2E = 1.4426950408889634

F32 = jnp.float32
BF16 = jnp.bfloat16
_NT = (((1,), (1,)), ((), ()))


def _params(*sem):
    return pltpu.CompilerParams(dimension_semantics=sem, vmem_limit_bytes=VMEM_LIMIT)


def _norm_kernel(x_ref, g_ref, o_ref):
    x = x_ref[...]
    r = lax.rsqrt(jnp.mean(x * x, axis=-1, keepdims=True) + NORM_EPS)
    o_ref[...] = (x * r * g_ref[...]).astype(o_ref.dtype)


def _rms_norm(x, g, tm):
    m = x.shape[0]
    return pl.pallas_call(
        _norm_kernel,
        grid=(m // tm,),
        in_specs=[pl.BlockSpec((tm, D_MODEL), lambda i: (i, 0)),
                  pl.BlockSpec((1, D_MODEL), lambda i: (0, 0))],
        out_specs=pl.BlockSpec((tm, D_MODEL), lambda i: (i, 0)),
        out_shape=jax.ShapeDtypeStruct((m, D_MODEL), BF16),
        compiler_params=_params("parallel"),
        name="rms_norm",
    )(x, g.reshape(1, D_MODEL))


PROJ_SUB_ROWS = 128
DIAG_PIECES = 4


def _proj_rows(a_ref, w, c_ref, s1_ref, s2_ref, o_ref, half):
    tm, tn = o_ref.shape
    sub = min(tm, PROJ_SUB_ROWS)
    for rb in range(tm // sub):
        rows = slice(rb * sub, (rb + 1) * sub)
        acc = jnp.dot(a_ref[rows, :], w, preferred_element_type=F32)
        c, s1, s2 = c_ref[rows, :], s1_ref[rows, :], s2_ref[rows, :]
        for cb in range(tn // LANES):
            x = acc[:, cb * LANES:(cb + 1) * LANES]
            y = (x * c + pltpu.roll(x, LANES - half, 1) * s1 + pltpu.roll(x, half, 1) * s2)
            o_ref[rows, cb * LANES:(cb + 1) * LANES] = y.astype(o_ref.dtype)


def _proj_kernel(*refs, half, with_meta):
    if with_meta:
        (a_ref, w_ref, c_ref, s1_ref, s2_ref, am_ref, cm_ref, s1m_ref, s2m_ref,
         o_ref, om_ref, wb_ref) = refs
    else:
        a_ref, w_ref, c_ref, s1_ref, s2_ref, o_ref, wb_ref = refs

    @pl.when(pl.program_id(1) == 0)
    def _():
        wb_ref[...] = w_ref[...].astype(wb_ref.dtype)
        if with_meta:
            _proj_rows(am_ref, wb_ref[...], cm_ref, s1m_ref, s2m_ref, om_ref, half)

    _proj_rows(a_ref, wb_ref[...], c_ref, s1_ref, s2_ref, o_ref, half)


def _proj(a, w, layer, tables, *, tm, tn, rope_cols, q_cols, half, a_meta=None, tables_meta=None):
    m, n = a.shape[0], w.shape[2]
    tab_blocks = (tables[0].shape[0] - tm) // (2 * tm)
    n_q_tiles, n_rope_tiles = q_cols // tn, rope_cols // tn
    with_meta = a_meta is not None

    def kind(j):
        return jnp.where(j < n_q_tiles, 0, jnp.where(j < n_rope_tiles, 1, 2))

    n_seq = (m // tm) // tab_blocks

    def row_tile(i):
        return (i % n_seq) * tab_blocks + i // n_seq

    tab_spec = pl.BlockSpec((tm, LANES), lambda j, i: (kind(j) * tab_blocks + jnp.where(
        kind(j) == 2, 0, i // n_seq), 0))
    in_specs = [pl.BlockSpec((tm, D_MODEL), lambda j, i: (row_tile(i), 0)),
                pl.BlockSpec((None, D_MODEL, tn), lambda j, i: (layer, 0, j)),
                tab_spec, tab_spec, tab_spec]
    out_specs = [pl.BlockSpec((tm, tn), lambda j, i: (row_tile(i), j))]
    out_shape = [jax.ShapeDtypeStruct((m, n), BF16)]
    operands = [a, w, *tables]
    if with_meta:
        mm = a_meta.shape[0]
        tabm_spec = pl.BlockSpec((mm, LANES), lambda j, i: (kind(j), 0))
        in_specs += [pl.BlockSpec((mm, D_MODEL), lambda j, i: (0, 0)), tabm_spec, tabm_spec, tabm_spec]
        out_specs.append(pl.BlockSpec((mm, tn), lambda j, i: (0, j)))
        out_shape.append(jax.ShapeDtypeStruct((mm, n), BF16))
        operands += [a_meta, *tables_meta]
    outs = pl.pallas_call(
        functools.partial(_proj_kernel, half=half, with_meta=with_meta),
        grid=(n // tn, m // tm),
        in_specs=in_specs,
        out_specs=out_specs,
        out_shape=out_shape,
        scratch_shapes=[pltpu.VMEM((D_MODEL, tn), BF16)],
        compiler_params=_params("arbitrary", "arbitrary"),
        name="proj_rope",
    )(*operands)
    return outs if with_meta else outs[0]


def _proj_tables(pos, head_dim, half, q_scale, tm):
    ident = (jnp.ones((tm, LANES), F32), jnp.zeros((tm, LANES), F32), jnp.zeros((tm, LANES), F32))
    return tuple(jnp.concatenate([t * q_scale, t, e], axis=0)
                 for t, e in zip(_rope_tables(pos, head_dim, half), ident))


def _rope_tables(pos, head_dim, half):
    inv = ROPE_THETA ** (-jnp.arange(0, 2 * half, 2, dtype=F32) / (2 * half))
    ang = pos.astype(F32)[:, None] * inv[None, :]
    cos, sin = jnp.cos(ang), jnp.sin(ang)
    p = pos.shape[0]
    ones = jnp.ones((p, head_dim - 2 * half), F32)
    zeros = jnp.zeros((p, head_dim - 2 * half), F32)
    zh = jnp.zeros((p, half), F32)
    c = jnp.concatenate([cos, cos, ones], axis=1)
    s1 = jnp.concatenate([-sin, zh, zeros], axis=1)
    s2 = jnp.concatenate([zh, sin, zeros], axis=1)
    rep = LANES // head_dim
    return tuple(jnp.tile(t, (1, rep)) for t in (c, s1, s2))


def _lambda_value(lam_ref, lambda_init):
    lv = lam_ref[...]
    return (jnp.exp(jnp.sum(lv[0:1] * lv[1:2], axis=-1, keepdims=True))
            - jnp.exp(jnp.sum(lv[2:3] * lv[3:4], axis=-1, keepdims=True)) + lambda_init)


def _diff_epilogue(acc, l1, l2, lam, g, gate, lambda_init):
    tq = acc.shape[0] // 2
    o = acc[:tq] * (1.0 / l1) - acc[tq:] * (lam / l2)
    r = lax.rsqrt(jnp.mean(o * o, axis=-1, keepdims=True) + NORM_EPS)
    o = o * r * (g * (1.0 - lambda_init))
    gate = gate.astype(F32)
    return o * (gate * jax.nn.sigmoid(gate))


def _lane_block_reduce(x, op):
    r = x[:, :LANES]
    for c in range(1, x.shape[1] // LANES):
        r = op(r, x[:, c * LANES:(c + 1) * LANES])
    return r


def _attn_a_kernel(lam_ref, dmat_ref, qa_ref, qb_ref, k_ref, v_ref, km_ref, vm_ref, ga_ref, gb_ref,
                   g_ref, o_ref, q_sc, s_ref, mr_ref, mb_ref, ls_ref, acc_ref, *, tq, n_tiles,
                   lambda_init):
    i = pl.program_id(2)
    hd = A_HEAD_DIM
    n_steps = n_tiles + 1
    hq = tq // DIAG_PIECES
    neg_inf = jnp.float32(-jnp.inf)
    q_sc[0] = qa_ref[...]
    q_sc[1] = qb_ref[...]
    km = km_ref[...]
    vm = vm_ref[...]
    lam = _lambda_value(lam_ref, lambda_init)
    vis = dmat_ref[...] <= 0

    def step_info(t):
        if t == 0:
            return 0, i, True
        if t == n_steps - 1:
            return 1, n_tiles - 1 - i, True
        if t > n_tiles // 2 - 1:
            return 1, t - i - 1, False
        low = t <= i
        return jnp.where(low, 0, 1), jnp.where(low, i - t, t - i - 1), False

    def parts(diag):
        if not diag:
            return ((slice(0, tq), tq),)
        return tuple((slice(r * hq, (r + 1) * hq), (r + 1) * hq) for r in range(DIAG_PIECES))

    init = jnp.full((tq, LANES), neg_inf, F32)
    for h in range(2):
        for sel in range(2):
            mr_ref[h, sel] = init
            ls_ref[h, sel] = jnp.zeros((tq, LANES), F32)
    acc_ref[...] = jnp.zeros(acc_ref.shape, F32)
    for t in range(n_steps):
        sel, kt, diag = step_info(t)
        krows = k_ref[pl.ds(pl.multiple_of(kt * tq, tq), tq), :]
        for h in range(2):
            for rows, nk in parts(diag):
                s = lax.dot_general(q_sc[sel, rows, h * hd:(h + 1) * hd],
                                    krows[:nk, h * hd:(h + 1) * hd], _NT,
                                    preferred_element_type=F32)
                if diag:
                    masked = jnp.where(vis, s[:, nk - hq:], neg_inf)
                    s = masked if nk == hq else jnp.concatenate([s[:, :nk - hq], masked], axis=1)
                s_ref[h, t, rows, :nk] = s
                mr_ref[h, sel, rows, :] = jnp.maximum(mr_ref[h, sel, rows, :],
                                                      _lane_block_reduce(s, jnp.maximum))

    lane = lax.broadcasted_iota(jnp.int32, km.shape, 1)
    head0 = jnp.where(lane < hd, 1.0, 0.0).astype(BF16)
    head1 = jnp.where(lane < hd, 0.0, 1.0).astype(BF16)
    km_bd = jnp.concatenate([km * head0, km * head1], axis=0)
    zero_v = jnp.zeros_like(vm)
    vm_bd = jnp.concatenate([jnp.concatenate([vm, zero_v], axis=1),
                             jnp.concatenate([zero_v, vm], axis=1)], axis=0)
    first = lax.broadcasted_iota(jnp.int32, (1, 2 * N_META), 1) < N_META
    sms = [lax.dot_general(q_sc[sel], km_bd, _NT, preferred_element_type=F32) for sel in range(2)]

    for sel in range(2):
        for h in range(2):
            own = first if h == 0 else jnp.logical_not(first)
            m = jnp.maximum(jnp.max(jnp.where(own, sms[sel], neg_inf), axis=-1, keepdims=True),
                            jnp.max(mr_ref[h, sel], axis=-1, keepdims=True))
            mb_ref[h, sel] = jnp.broadcast_to(m, (tq, LANES))

    for t in range(n_steps):
        sel, kt, diag = step_info(t)
        vrows = v_ref[pl.ds(pl.multiple_of(kt * tq, tq), tq), :]
        for rows, nk in parts(diag):
            ps = []
            for h in range(2):
                mb = mb_ref[h, sel, rows, :]
                blocks = []
                ls = None
                for c in range(nk // LANES):
                    e = jnp.exp2(s_ref[h, t, rows, c * LANES:(c + 1) * LANES] - mb)
                    ls = e if ls is None else ls + e
                    blocks.append(e.astype(BF16))
                ls_ref[h, sel, rows, :] = ls_ref[h, sel, rows, :] + ls
                ps.append(jnp.concatenate(blocks, axis=1))
            pv = jnp.dot(jnp.concatenate(ps, axis=0), vrows[:nk], preferred_element_type=F32)
            n = rows.stop - rows.start
            for h in range(2):
                arows = slice(h * tq + rows.start, h * tq + rows.stop)
                acc_ref[sel, arows, :] = acc_ref[sel, arows, :] + pv[h * n:(h + 1) * n]

    for sel, gate_ref in ((0, ga_ref), (1, gb_ref)):
        em = jnp.exp2(sms[sel] - jnp.where(first, mb_ref[0, sel, :, :1], mb_ref[1, sel, :, :1]))
        ls = [jnp.sum(ls_ref[0, sel], axis=-1, keepdims=True)
              + jnp.sum(jnp.where(first, em, 0.0), axis=-1, keepdims=True),
              jnp.sum(ls_ref[1, sel], axis=-1, keepdims=True)
              + jnp.sum(jnp.where(first, 0.0, em), axis=-1, keepdims=True)]
        pv = jnp.dot(em.astype(BF16), vm_bd, preferred_element_type=F32)
        acc = acc_ref[sel] + jnp.concatenate([pv[:, :2 * hd], pv[:, 2 * hd:]], axis=0)
        o = _diff_epilogue(acc, ls[0], ls[1], lam, g_ref[...], gate_ref[...], lambda_init)
        o_ref[sel] = o.astype(o_ref.dtype)


def _attn_a(qkvg, qkvg_meta, lam_vecs, subln_g, *, batch, seq, tq, lambda_init):
    n_tiles = seq // tq
    half = n_tiles // 2
    hw = 2 * A_HEAD_DIM
    nh = A_HEADS
    piece = tq // DIAG_PIECES
    idx = jnp.arange(piece, dtype=jnp.int32) >> CHUNK_SHIFT
    dmat = idx[None, :] - idx[:, None]
    kern = functools.partial(_attn_a_kernel, tq=tq, n_tiles=n_tiles, lambda_init=lambda_init)
    lo = lambda b, h, i: b * n_tiles + i
    hi = lambda b, h, i: b * n_tiles + n_tiles - 1 - i
    return pl.pallas_call(
        kern,
        grid=(batch, nh, half),
        in_specs=[
            pl.BlockSpec((4, A_HEAD_DIM), lambda b, h, i: (0, 0)),
            pl.BlockSpec((piece, piece), lambda b, h, i: (0, 0)),
            pl.BlockSpec((tq, hw), lambda b, h, i: (lo(b, h, i), h)),
            pl.BlockSpec((tq, hw), lambda b, h, i: (hi(b, h, i), h)),
            pl.BlockSpec((seq, hw), lambda b, h, i: (b, nh + h)),
            pl.BlockSpec((seq, hw), lambda b, h, i: (b, 2 * nh + h)),
            pl.BlockSpec((N_META, hw), lambda b, h, i: (0, nh + h)),
            pl.BlockSpec((N_META, hw), lambda b, h, i: (0, 2 * nh + h)),
            pl.BlockSpec((tq, hw), lambda b, h, i: (lo(b, h, i), 3 * nh + h)),
            pl.BlockSpec((tq, hw), lambda b, h, i: (hi(b, h, i), 3 * nh + h)),
            pl.BlockSpec((1, hw), lambda b, h, i: (0, 0)),
        ],
        out_specs=pl.BlockSpec((2, None, None, tq, hw), lambda b, h, i: (0, b, i, 0, h)),
        out_shape=jax.ShapeDtypeStruct((2, batch, half, tq, D_MODEL), BF16),
        scratch_shapes=[pltpu.VMEM((2, tq, hw), BF16),
                        pltpu.VMEM((2, n_tiles + 1, tq, tq), F32),
                        pltpu.VMEM((2, 2, tq, LANES), F32),
                        pltpu.VMEM((2, 2, tq, LANES), F32),
                        pltpu.VMEM((2, 2, tq, LANES), F32),
                        pltpu.VMEM((2, 2 * tq, hw), F32)],
        compiler_params=_params("parallel", "parallel", "arbitrary"),
        name="diff_attn",
    )(lam_vecs, dmat, qkvg, qkvg, qkvg, qkvg, qkvg_meta, qkvg_meta, qkvg, qkvg,
      subln_g.reshape(1, hw))


def _attn_a_row_block(r, *, batch, seq, tq, tm):
    n_tiles = seq // tq
    half = n_tiles // 2
    sub = tq // tm
    b = r // (n_tiles * sub)
    u = r % (n_tiles * sub)
    tile, within = u // sub, u % sub
    hi = tile // half
    pos = jnp.where(hi == 0, tile, n_tiles - 1 - tile)
    return ((hi * batch + b) * half + pos) * sub + within


def _attn_a_meta_kernel(lam_ref, q_ref, km_ref, vm_ref, gate_ref, g_ref, o_ref, *, lambda_init):
    hd = A_HEAD_DIM
    q = q_ref[...]
    km = km_ref[...]
    ems, ls = [], []
    for h in range(2):
        sm = lax.dot_general(q[:, h * hd:(h + 1) * hd], km[:, h * hd:(h + 1) * hd], _NT,
                             preferred_element_type=F32)
        em = jnp.exp2(sm - jnp.max(sm, axis=-1, keepdims=True))
        ls.append(jnp.sum(em, axis=-1, keepdims=True))
        ems.append(em.astype(BF16))
    acc = jnp.dot(jnp.concatenate(ems, axis=0), vm_ref[...], preferred_element_type=F32)
    lam = _lambda_value(lam_ref, lambda_init)
    o = _diff_epilogue(acc, ls[0], ls[1], lam, g_ref[...], gate_ref[...], lambda_init)
    o_ref[...] = o.astype(o_ref.dtype)


def _attn_a_meta(qkvg_meta, lam_vecs, subln_g, *, lambda_init):
    hw = 2 * A_HEAD_DIM
    nh = A_HEADS
    kern = functools.partial(_attn_a_meta_kernel, lambda_init=lambda_init)
    return pl.pallas_call(
        kern,
        grid=(nh,),
        in_specs=[
            pl.BlockSpec((4, A_HEAD_DIM), lambda h: (0, 0)),
            pl.BlockSpec((META_ROWS, hw), lambda h: (0, h)),
            pl.BlockSpec((N_META, hw), lambda h: (0, nh + h)),
            pl.BlockSpec((N_META, hw), lambda h: (0, 2 * nh + h)),
            pl.BlockSpec((META_ROWS, hw), lambda h: (0, 3 * nh + h)),
            pl.BlockSpec((1, hw), lambda h: (0, 0)),
        ],
        out_specs=pl.BlockSpec((META_ROWS, hw), lambda h: (0, h)),
        out_shape=jax.ShapeDtypeStruct((META_ROWS, D_MODEL), BF16),
        compiler_params=_params("parallel"),
        name="diff_attn_meta",
    )(lam_vecs, qkvg_meta, qkvg_meta, qkvg_meta, qkvg_meta, subln_g.reshape(1, hw))


SWA_KEYS = 256
SWA_SINK_COL = N_META


def _swa_kernel(sink_ref, q_ref, kd_ref, vd_ref, kdm_ref, vdm_ref, gate_ref, o_ref, *, n_chunks):
    step = pl.program_id(1)
    tq = CHUNK
    win = 3 * CHUNK
    nk = SWA_KEYS
    neg_inf = jnp.float32(-jnp.inf)

    colh = lax.broadcasted_iota(jnp.int32, (1, 2 * nk), 1) & (nk - 1)
    lane = lax.broadcasted_iota(jnp.int32, (nk, LANES), 1)
    lo_mask = jnp.where(lane < B_HEAD_DIM, 1.0, 0.0).astype(BF16)
    hi_mask = jnp.where(lane < B_HEAD_DIM, 0.0, 1.0).astype(BF16)
    pad = jnp.zeros((CHUNK - N_META, LANES), BF16)
    indicator = jnp.concatenate([lo_mask, hi_mask], axis=0)
    sink_rows = sink_ref[...] * LOG2E

    def block_diag(meta_rows, win_rows):
        w = jnp.concatenate([meta_rows, pad, win_rows], axis=0)
        return jnp.concatenate([w * lo_mask, w * hi_mask], axis=0)

    def chunk_body(cc, carry):
        c = step * n_chunks + cc
        r0 = pl.multiple_of(cc * CHUNK, CHUNK)
        start = pl.multiple_of(jnp.maximum(c * CHUNK - 2 * CHUNK, 0), CHUNK)
        n_vis = (jnp.minimum(c, 2) + 1) * CHUNK
        valid = jnp.logical_or(colh <= SWA_SINK_COL,
                               jnp.logical_and(colh >= CHUNK, colh < CHUNK + n_vis))
        for kv in range(B_KV_HEADS):
            cs = slice(kv * LANES, (kv + 1) * LANES)
            kbd = block_diag(kdm_ref[:, cs], kd_ref[pl.ds(start, win), cs])
            vbd = jnp.concatenate([block_diag(vdm_ref[:, cs], vd_ref[pl.ds(start, win), cs]),
                                   indicator], axis=1)
            q4 = jnp.concatenate(
                [q_ref[pl.ds(r0, tq), kv * 512 + p * LANES: kv * 512 + (p + 1) * LANES]
                 for p in range(4)], axis=0)
            s = lax.dot_general(q4, kbd, _NT, preferred_element_type=F32)
            p_rows = []
            for p in range(4):
                bias = jnp.where(valid, sink_rows[kv * 4 + p:kv * 4 + p + 1, :], neg_inf)
                sp = s[p * tq:(p + 1) * tq] + bias
                p_cols = []
                for hp in range(2):
                    sq = sp[:, hp * nk:(hp + 1) * nk]
                    e = jnp.exp2(sq - jnp.max(sq, axis=-1, keepdims=True))
                    p_cols.append(e.astype(BF16))
                p_rows.append(jnp.concatenate(p_cols, axis=1))
            pmat = jnp.concatenate(p_rows, axis=0)
            o4 = jnp.dot(pmat, vbd, preferred_element_type=F32)
            on = o4[:, :LANES] * (1.0 / o4[:, LANES:])
            for p in range(4):
                osl = slice(kv * 512 + p * LANES, kv * 512 + (p + 1) * LANES)
                gate = gate_ref[pl.ds(r0, tq), osl].astype(F32)
                o_ref[pl.ds(r0, tq), osl] = (on[p * tq:(p + 1) * tq]
                                             * (gate * jax.nn.sigmoid(gate))).astype(o_ref.dtype)
        return carry

    lax.fori_loop(0, n_chunks, chunk_body, 0, unroll=8)


def _swa(qg, kvd, kvd_meta, sinks, *, batch, seq, n_chunks):
    ns = seq // (CHUNK * n_chunks)
    rows = CHUNK * n_chunks
    kw = B_KV_HEADS * LANES
    sink_rows = jnp.zeros((B_KV_HEADS * 4, 2, SWA_KEYS), F32).at[:, :, SWA_SINK_COL].set(
        sinks.astype(F32).reshape(B_KV_HEADS * 4, 2)).reshape(B_KV_HEADS * 4, 2 * SWA_KEYS)
    return pl.pallas_call(
        functools.partial(_swa_kernel, n_chunks=n_chunks),
        grid=(batch, ns),
        in_specs=[
            pl.BlockSpec((B_KV_HEADS * 4, 2 * SWA_KEYS), lambda b, c: (0, 0)),
            pl.BlockSpec((rows, D_MODEL), lambda b, c: (b * ns + c, 0)),
            pl.BlockSpec((seq, kw), lambda b, c: (b, 0)),
            pl.BlockSpec((seq, kw), lambda b, c: (b, 1)),
            pl.BlockSpec((N_META, kw), lambda b, c: (0, 0)),
            pl.BlockSpec((N_META, kw), lambda b, c: (0, 1)),
            pl.BlockSpec((rows, D_MODEL), lambda b, c: (b * ns + c, 1)),
        ],
        out_specs=pl.BlockSpec((rows, D_MODEL), lambda b, c: (b * ns + c, 0)),
        out_shape=jax.ShapeDtypeStruct((batch * seq, D_MODEL), BF16),
        compiler_params=_params("parallel", "arbitrary"),
        name="swa_attn",
    )(sink_rows, qg, kvd, kvd, kvd_meta, kvd_meta, qg)


def _out_kernel(*refs, n_norm, emit_h):
    o_ref, w_ref, h_ref = refs[:3]
    g_refs = refs[3:3 + n_norm]
    outs = refs[3 + n_norm:]
    tm = o_ref.shape[0]
    sub = min(tm, PROJ_SUB_ROWS)
    w = w_ref[...]
    for rb in range(tm // sub):
        rows = slice(rb * sub, (rb + 1) * sub)
        acc = jnp.dot(o_ref[rows, :], w, preferred_element_type=F32) + h_ref[rows, :]
        norm_outs = outs
        if emit_h:
            outs[0][rows, :] = acc
            norm_outs = outs[1:]
        r = lax.rsqrt(jnp.mean(acc * acc, axis=-1, keepdims=True) + NORM_EPS)
        xn = acc * r
        for g_ref, n_ref in zip(g_refs, norm_outs):
            n_ref[rows, :] = (xn * g_ref[...]).astype(n_ref.dtype)


def _out_proj(o, w, h, gains, *, tm, emit_h, norm_dtype, o_row_block=None):
    m = h.shape[0]
    row = pl.BlockSpec((tm, D_MODEL), lambda i: (i, 0))
    o_spec = row if o_row_block is None else pl.BlockSpec((tm, D_MODEL), lambda i: (o_row_block(i), 0))
    gspec = pl.BlockSpec((1, D_MODEL), lambda i: (0, 0))
    out_shape = ([jax.ShapeDtypeStruct((m, D_MODEL), F32)] if emit_h else []) + [
        jax.ShapeDtypeStruct((m, D_MODEL), norm_dtype) for _ in gains]
    kern = functools.partial(_out_kernel, n_norm=len(gains), emit_h=emit_h)
    return pl.pallas_call(
        kern,
        grid=(m // tm,),
        in_specs=[o_spec, pl.BlockSpec((D_MODEL, D_MODEL), lambda i: (0, 0)), row] + [gspec] * len(gains),
        out_specs=[row] * len(out_shape),
        out_shape=out_shape,
        compiler_params=_params("parallel"),
        name="out_proj",
    )(o, w, h, *[g.reshape(1, D_MODEL) for g in gains])


def kernel(x, meta_tokens, a_norm, a_w_in, a_w_out, a_lambda_q1, a_lambda_k1, a_lambda_q2,
           a_lambda_k2, a_subln, kv_norm, w_kv, b_norm, b_w_in, b_w_out, b_sinks, final_norm):
    batch, seq, d = x.shape
    n_a = a_w_in.shape[0]
    n_b = b_w_in.shape[0]
    h_r = x.reshape(batch * seq, d)
    h_m = jnp.concatenate([meta_tokens.astype(x.dtype),
                           jnp.zeros((META_ROWS - N_META, d), x.dtype)], axis=0)

    tm_r = 1024
    tm_out = 512
    tq_a = 512
    a_scale = A_HEAD_DIM ** -0.5 * LOG2E
    b_scale = B_HEAD_DIM ** -0.5 * LOG2E
    attn_rows = functools.partial(_attn_a_row_block, batch=batch, seq=seq, tq=tq_a, tm=tm_out)

    pos_r = jnp.arange(seq, dtype=jnp.int32) + N_META
    pos_m = jnp.arange(META_ROWS, dtype=jnp.int32)
    tab_a_r = _proj_tables(pos_r, A_HEAD_DIM, A_ROT_HALF, a_scale, tm_r)
    tab_a_m = _proj_tables(pos_m, A_HEAD_DIM, A_ROT_HALF, a_scale, META_ROWS)
    tab_b_r = _proj_tables(pos_r, B_HEAD_DIM, B_ROT_HALF, b_scale, tm_r)
    tab_b_m = _proj_tables(pos_m, B_HEAD_DIM, B_ROT_HALF, b_scale, META_ROWS)

    wk, wv = jnp.split(w_kv, 2, axis=1)

    def dup_heads(w):
        w = w.reshape(d, B_KV_HEADS, 1, B_HEAD_DIM)
        return jnp.broadcast_to(w, (d, B_KV_HEADS, 2, B_HEAD_DIM)).reshape(d, B_KV_HEADS * LANES)

    w_kvd = jnp.concatenate([dup_heads(wk), dup_heads(wv)], axis=1)

    hn_r = _rms_norm(h_r, a_norm[0], 512)
    hn_m = _rms_norm(h_m, a_norm[0], META_ROWS)
    for l in range(n_a):
        lambda_init = 0.8 - 0.6 * math.exp(-0.3 * l)
        proj_kw = dict(tn=1024, rope_cols=2 * d, q_cols=d, half=A_ROT_HALF)
        w_out = a_w_out[l].astype(BF16)
        qkvg_r, qkvg_m = _proj(hn_r, a_w_in, l, tab_a_r, tm=tm_r, a_meta=hn_m, tables_meta=tab_a_m,
                                **proj_kw)
        lam_vecs = jnp.stack([a_lambda_q1[l], a_lambda_k1[l], a_lambda_q2[l], a_lambda_k2[l]]).astype(F32)
        o_r = _attn_a(qkvg_r, qkvg_m, lam_vecs, a_subln[l], batch=batch, seq=seq, tq=tq_a,
                      lambda_init=lambda_init)
        o_m = _attn_a_meta(qkvg_m, lam_vecs, a_subln[l], lambda_init=lambda_init)
        if l + 1 < n_a:
            gains_r = gains_m = [a_norm[l + 1]]
        else:
            gains_r, gains_m = [kv_norm, b_norm[0]], [kv_norm]
        outs_r = _out_proj(o_r.reshape(batch * seq, d), w_out, h_r, gains_r, tm=tm_out,
                           emit_h=True, norm_dtype=BF16, o_row_block=attn_rows)
        outs_m = _out_proj(o_m, w_out, h_m, gains_m, tm=META_ROWS, emit_h=True, norm_dtype=BF16)
        h_r, h_m = outs_r[0], outs_m[0]
        hn_r, hn_m = outs_r[1], outs_m[1]

    kv_kw = dict(tn=512, rope_cols=B_KV_HEADS * LANES, q_cols=0, half=B_ROT_HALF)
    kvd_r, kvd_m = _proj(hn_r, w_kvd[None], 0, tab_b_r, tm=tm_r, a_meta=hn_m, tables_meta=tab_b_m,
                         **kv_kw)
    hn_r = outs_r[2]

    for l in range(n_b):
        qg = _proj(hn_r, b_w_in, l, tab_b_r, tm=tm_r, tn=1024, rope_cols=d, q_cols=d,
                   half=B_ROT_HALF)
        o_r = _swa(qg, kvd_r, kvd_m, b_sinks[l], batch=batch, seq=seq, n_chunks=8)
        w_out = b_w_out[l].astype(BF16)
        if l + 1 < n_b:
            h_r, hn_r = _out_proj(o_r, w_out, h_r, [b_norm[l + 1]], tm=tm_out, emit_h=True,
                                  norm_dtype=BF16)
        else:
            (out,) = _out_proj(o_r, w_out, h_r, [final_norm], tm=tm_out, emit_h=False,
                               norm_dtype=x.dtype)
    return out.reshape(batch, seq, d)
```

```python
import functools
import math

import jax
import jax.numpy as jnp
from jax import lax
from jax.experimental import pallas as pl
from jax.experimental.pallas import tpu as pltpu

D_MODEL = 2048
N_META = 16
META_ROWS = 64
CHUNK = 64
CHUNK_SHIFT = 6
ROPE_THETA = 500000.0
NORM_EPS = 1e-5
A_HEADS = 8
A_HEAD_DIM = 128
A_ROT_HALF = 16
B_HEAD_DIM = 64
B_KV_HEADS = 4
B_ROT_HALF = 8
LANES = 128
VMEM_LIMIT = 56 * 1024 * 1024
LOG2E = 1.4426950408889634

F32 = jnp.float32
BF16 = jnp.bfloat16
_NT = (((1,), (1,)), ((), ()))


def _params(*sem):
    return pltpu.CompilerParams(dimension_semantics=sem, vmem_limit_bytes=VMEM_LIMIT)


def _norm_kernel(x_ref, g_ref, o_ref):
    x = x_ref[...]
    r = lax.rsqrt(jnp.mean(x * x, axis=-1, keepdims=True) + NORM_EPS)
    o_ref[...] = (x * r * g_ref[...]).astype(o_ref.dtype)


def _rms_norm(x, g, tm):
    m = x.shape[0]
    return pl.pallas_call(
        _norm_kernel,
        grid=(m // tm,),
        in_specs=[pl.BlockSpec((tm, D_MODEL), lambda i: (i, 0)),
                  pl.BlockSpec((1, D_MODEL), lambda i: (0, 0))],
        out_specs=pl.BlockSpec((tm, D_MODEL), lambda i: (i, 0)),
        out_shape=jax.ShapeDtypeStruct((m, D_MODEL), BF16),
        compiler_params=_params("parallel"),
        name="rms_norm",
    )(x, g.reshape(1, D_MODEL))


PROJ_SUB_ROWS = 128
DIAG_PIECES = 4


def _proj_rows(a_ref, w, c_ref, s1_ref, s2_ref, o_ref, half):
    tm, tn = o_ref.shape
    sub = min(tm, PROJ_SUB_ROWS)
    for rb in range(tm // sub):
        rows = slice(rb * sub, (rb + 1) * sub)
        acc = jnp.dot(a_ref[rows, :], w, preferred_element_type=F32)
        c, s1, s2 = c_ref[rows, :], s1_ref[rows, :], s2_ref[rows, :]
        for cb in range(tn // LANES):
            x = acc[:, cb * LANES:(cb + 1) * LANES]
            y = (x * c + pltpu.roll(x, LANES - half, 1) * s1 + pltpu.roll(x, half, 1) * s2)
            o_ref[rows, cb * LANES:(cb + 1) * LANES] = y.astype(o_ref.dtype)


def _proj_kernel(*refs, half, with_meta):
    if with_meta:
        (a_ref, w_ref, c_ref, s1_ref, s2_ref, am_ref, cm_ref, s1m_ref, s2m_ref,
         o_ref, om_ref, wb_ref) = refs
    else:
        a_ref, w_ref, c_ref, s1_ref, s2_ref, o_ref, wb_ref = refs

    @pl.when(pl.program_id(1) == 0)
    def _():
        wb_ref[...] = w_ref[...].astype(wb_ref.dtype)
        if with_meta:
            _proj_rows(am_ref, wb_ref[...], cm_ref, s1m_ref, s2m_ref, om_ref, half)

    _proj_rows(a_ref, wb_ref[...], c_ref, s1_ref, s2_ref, o_ref, half)


def _proj(a, w, layer, tables, *, tm, tn, rope_cols, q_cols, half, a_meta=None, tables_meta=None):
    m, n = a.shape[0], w.shape[2]
    tab_blocks = (tables[0].shape[0] - tm) // (2 * tm)
    n_q_tiles, n_rope_tiles = q_cols // tn, rope_cols // tn
    with_meta = a_meta is not None

    def kind(j):
        return jnp.where(j < n_q_tiles, 0, jnp.where(j < n_rope_tiles, 1, 2))

    n_seq = (m // tm) // tab_blocks

    def row_tile(i):
        return (i % n_seq) * tab_blocks + i // n_seq

    tab_spec = pl.BlockSpec((tm, LANES), lambda j, i: (kind(j) * tab_blocks + jnp.where(
        kind(j) == 2, 0, i // n_seq), 0))
    in_specs = [pl.BlockSpec((tm, D_MODEL), lambda j, i: (row_tile(i), 0)),
                pl.BlockSpec((None, D_MODEL, tn), lambda j, i: (layer, 0, j)),
                tab_spec, tab_spec, tab_spec]
    out_specs = [pl.BlockSpec((tm, tn), lambda j, i: (row_tile(i), j))]
    out_shape = [jax.ShapeDtypeStruct((m, n), BF16)]
    operands = [a, w, *tables]
    if with_meta:
        mm = a_meta.shape[0]
        tabm_spec = pl.BlockSpec((mm, LANES), lambda j, i: (kind(j), 0))
        in_specs += [pl.BlockSpec((mm, D_MODEL), lambda j, i: (0, 0)), tabm_spec, tabm_spec, tabm_spec]
        out_specs.append(pl.BlockSpec((mm, tn), lambda j, i: (0, j)))
        out_shape.append(jax.ShapeDtypeStruct((mm, n), BF16))
        operands += [a_meta, *tables_meta]
    outs = pl.pallas_call(
        functools.partial(_proj_kernel, half=half, with_meta=with_meta),
        grid=(n // tn, m // tm),
        in_specs=in_specs,
        out_specs=out_specs,
        out_shape=out_shape,
        scratch_shapes=[pltpu.VMEM((D_MODEL, tn), BF16)],
        compiler_params=_params("arbitrary", "arbitrary"),
        name="proj_rope",
    )(*operands)
    return outs if with_meta else outs[0]


def _proj_tables(pos, head_dim, half, q_scale, tm):
    ident = (jnp.ones((tm, LANES), F32), jnp.zeros((tm, LANES), F32), jnp.zeros((tm, LANES), F32))
    return tuple(jnp.concatenate([t * q_scale, t, e], axis=0)
                 for t, e in zip(_rope_tables(pos, head_dim, half), ident))


def _rope_tables(pos, head_dim, half):
    inv = ROPE_THETA ** (-jnp.arange(0, 2 * half, 2, dtype=F32) / (2 * half))
    ang = pos.astype(F32)[:, None] * inv[None, :]
    cos, sin = jnp.cos(ang), jnp.sin(ang)
    p = pos.shape[0]
    ones = jnp.ones((p, head_dim - 2 * half), F32)
    zeros = jnp.zeros((p, head_dim - 2 * half), F32)
    zh = jnp.zeros((p, half), F32)
    c = jnp.concatenate([cos, cos, ones], axis=1)
    s1 = jnp.concatenate([-sin, zh, zeros], axis=1)
    s2 = jnp.concatenate([zh, sin, zeros], axis=1)
    rep = LANES // head_dim
    return tuple(jnp.tile(t, (1, rep)) for t in (c, s1, s2))


def _lambda_value(lam_ref, lambda_init):
    lv = lam_ref[...]
    return (jnp.exp(jnp.sum(lv[0:1] * lv[1:2], axis=-1, keepdims=True))
            - jnp.exp(jnp.sum(lv[2:3] * lv[3:4], axis=-1, keepdims=True)) + lambda_init)


def _diff_epilogue(acc, l1, l2, lam, g, gate, lambda_init):
    tq = acc.shape[0] // 2
    o = acc[:tq] * (1.0 / l1) - acc[tq:] * (lam / l2)
    r = lax.rsqrt(jnp.mean(o * o, axis=-1, keepdims=True) + NORM_EPS)
    o = o * r * (g * (1.0 - lambda_init))
    gate = gate.astype(F32)
    return o * (gate * jax.nn.sigmoid(gate))


def _lane_block_reduce(x, op):
    r = x[:, :LANES]
    for c in range(1, x.shape[1] // LANES):
        r = op(r, x[:, c * LANES:(c + 1) * LANES])
    return r


def _attn_a_kernel(lam_ref, dmat_ref, qa_ref, qb_ref, k_ref, v_ref, km_ref, vm_ref, ga_ref, gb_ref,
                   g_ref, o_ref, q_sc, s_ref, mr_ref, mb_ref, ls_ref, acc_ref, *, tq, n_tiles,
                   lambda_init):
    i = pl.program_id(2)
    hd = A_HEAD_DIM
    n_steps = n_tiles + 1
    hq = tq // DIAG_PIECES
    neg_inf = jnp.float32(-jnp.inf)
    q_sc[0] = qa_ref[...]
    q_sc[1] = qb_ref[...]
    km = km_ref[...]
    vm = vm_ref[...]
    lam = _lambda_value(lam_ref, lambda_init)
    vis = dmat_ref[...] <= 0

    def step_info(t):
        if t == 0:
            return 0, i, True
        if t == n_steps - 1:
            return 1, n_tiles - 1 - i, True
        if t > n_tiles // 2 - 1:
            return 1, t - i - 1, False
        low = t <= i
        return jnp.where(low, 0, 1), jnp.where(low, i - t, t - i - 1), False

    def parts(diag):
        if not diag:
            return ((slice(0, tq), tq),)
        return tuple((slice(r * hq, (r + 1) * hq), (r + 1) * hq) for r in range(DIAG_PIECES))

    init = jnp.full((tq, LANES), neg_inf, F32)
    for h in range(2):
        for sel in range(2):
            mr_ref[h, sel] = init
            ls_ref[h, sel] = jnp.zeros((tq, LANES), F32)
    acc_ref[...] = jnp.zeros(acc_ref.shape, F32)
    for t in range(n_steps):
        sel, kt, diag = step_info(t)
        krows = k_ref[pl.ds(pl.multiple_of(kt * tq, tq), tq), :]
        for h in range(2):
            for rows, nk in parts(diag):
                s = lax.dot_general(q_sc[sel, rows, h * hd:(h + 1) * hd],
                                    krows[:nk, h * hd:(h + 1) * hd], _NT,
                                    preferred_element_type=F32)
                if diag:
                    masked = jnp.where(vis, s[:, nk - hq:], neg_inf)
                    s = masked if nk == hq else jnp.concatenate([s[:, :nk - hq], masked], axis=1)
                s_ref[h, t, rows, :nk] = s
                mr_ref[h, sel, rows, :] = jnp.maximum(mr_ref[h, sel, rows, :],
                                                      _lane_block_reduce(s, jnp.maximum))

    lane = lax.broadcasted_iota(jnp.int32, km.shape, 1)
    head0 = jnp.where(lane < hd, 1.0, 0.0).astype(BF16)
    head1 = jnp.where(lane < hd, 0.0, 1.0).astype(BF16)
    km_bd = jnp.concatenate([km * head0, km * head1], axis=0)
    zero_v = jnp.zeros_like(vm)
    vm_bd = jnp.concatenate([jnp.concatenate([vm, zero_v], axis=1),
                             jnp.concatenate([zero_v, vm], axis=1)], axis=0)
    first = lax.broadcasted_iota(jnp.int32, (1, 2 * N_META), 1) < N_META
    sms = [lax.dot_general(q_sc[sel], km_bd, _NT, preferred_element_type=F32) for sel in range(2)]

    for sel in range(2):
        for h in range(2):
            own = first if h == 0 else jnp.logical_not(first)
            m = jnp.maximum(jnp.max(jnp.where(own, sms[sel], neg_inf), axis=-1, keepdims=True),
                            jnp.max(mr_ref[h, sel], axis=-1, keepdims=True))
            mb_ref[h, sel] = jnp.broadcast_to(m, (tq, LANES))

    for t in range(n_steps):
        sel, kt, diag = step_info(t)
        vrows = v_ref[pl.ds(pl.multiple_of(kt * tq, tq), tq), :]
        for rows, nk in parts(diag):
            ps = []
            for h in range(2):
                mb = mb_ref[h, sel, rows, :]
                blocks = []
                ls = None
                for c in range(nk // LANES):
                    e = jnp.exp2(s_ref[h, t, rows, c * LANES:(c + 1) * LANES] - mb)
                    ls = e if ls is None else ls + e
                    blocks.append(e.astype(BF16))
                ls_ref[h, sel, rows, :] = ls_ref[h, sel, rows, :] + ls
                ps.append(jnp.concatenate(blocks, axis=1))
            pv = jnp.dot(jnp.concatenate(ps, axis=0), vrows[:nk], preferred_element_type=F32)
            n = rows.stop - rows.start
            for h in range(2):
                arows = slice(h * tq + rows.start, h * tq + rows.stop)
                acc_ref[sel, arows, :] = acc_ref[sel, arows, :] + pv[h * n:(h + 1) * n]

    for sel, gate_ref in ((0, ga_ref), (1, gb_ref)):
        em = jnp.exp2(sms[sel] - jnp.where(first, mb_ref[0, sel, :, :1], mb_ref[1, sel, :, :1]))
        ls = [jnp.sum(ls_ref[0, sel], axis=-1, keepdims=True)
              + jnp.sum(jnp.where(first, em, 0.0), axis=-1, keepdims=True),
              jnp.sum(ls_ref[1, sel], axis=-1, keepdims=True)
              + jnp.sum(jnp.where(first, 0.0, em), axis=-1, keepdims=True)]
        pv = jnp.dot(em.astype(BF16), vm_bd, preferred_element_type=F32)
        acc = acc_ref[sel] + jnp.concatenate([pv[:, :2 * hd], pv[:, 2 * hd:]], axis=0)
        o = _diff_epilogue(acc, ls[0], ls[1], lam, g_ref[...], gate_ref[...], lambda_init)
        o_ref[sel] = o.astype(o_ref.dtype)


def _attn_a(qkvg, qkvg_meta, lam_vecs, subln_g, *, batch, seq, tq, lambda_init):
    n_tiles = seq // tq
    half = n_tiles // 2
    hw = 2 * A_HEAD_DIM
    nh = A_HEADS
    piece = tq // DIAG_PIECES
    idx = jnp.arange(piece, dtype=jnp.int32) >> CHUNK_SHIFT
    dmat = idx[None, :] - idx[:, None]
    kern = functools.partial(_attn_a_kernel, tq=tq, n_tiles=n_tiles, lambda_init=lambda_init)
    lo = lambda b, h, i: b * n_tiles + i
    hi = lambda b, h, i: b * n_tiles + n_tiles - 1 - i
    return pl.pallas_call(
        kern,
        grid=(batch, nh, half),
        in_specs=[
            pl.BlockSpec((4, A_HEAD_DIM), lambda b, h, i: (0, 0)),
            pl.BlockSpec((piece, piece), lambda b, h, i: (0, 0)),
            pl.BlockSpec((tq, hw), lambda b, h, i: (lo(b, h, i), h)),
            pl.BlockSpec((tq, hw), lambda b, h, i: (hi(b, h, i), h)),
            pl.BlockSpec((seq, hw), lambda b, h, i: (b, nh + h)),
            pl.BlockSpec((seq, hw), lambda b, h, i: (b, 2 * nh + h)),
            pl.BlockSpec((N_META, hw), lambda b, h, i: (0, nh + h)),
            pl.BlockSpec((N_META, hw), lambda b, h, i: (0, 2 * nh + h)),
            pl.BlockSpec((tq, hw), lambda b, h, i: (lo(b, h, i), 3 * nh + h)),
            pl.BlockSpec((tq, hw), lambda b, h, i: (hi(b, h, i), 3 * nh + h)),
            pl.BlockSpec((1, hw), lambda b, h, i: (0, 0)),
        ],
        out_specs=pl.BlockSpec((2, None, None, tq, hw), lambda b, h, i: (0, b, i, 0, h)),
        out_shape=jax.ShapeDtypeStruct((2, batch, half, tq, D_MODEL), BF16),
        scratch_shapes=[pltpu.VMEM((2, tq, hw), BF16),
                        pltpu.VMEM((2, n_tiles + 1, tq, tq), F32),
                        pltpu.VMEM((2, 2, tq, LANES), F32),
                        pltpu.VMEM((2, 2, tq, LANES), F32),
                        pltpu.VMEM((2, 2, tq, LANES), F32),
                        pltpu.VMEM((2, 2 * tq, hw), F32)],
        compiler_params=_params("parallel", "parallel", "arbitrary"),
        name="diff_attn",
    )(lam_vecs, dmat, qkvg, qkvg, qkvg, qkvg, qkvg_meta, qkvg_meta, qkvg, qkvg,
      subln_g.reshape(1, hw))


def _attn_a_row_block(r, *, batch, seq, tq, tm):
    n_tiles = seq // tq
    half = n_tiles // 2
    sub = tq // tm
    b = r // (n_tiles * sub)
    u = r % (n_tiles * sub)
    tile, within = u // sub, u % sub
    hi = tile // half
    pos = jnp.where(hi == 0, tile, n_tiles - 1 - tile)
    return ((hi * batch + b) * half + pos) * sub + within


def _attn_a_meta_kernel(lam_ref, q_ref, km_ref, vm_ref, gate_ref, g_ref, o_ref, *, lambda_init):
    hd = A_HEAD_DIM
    q = q_ref[...]
    km = km_ref[...]
    ems, ls = [], []
    for h in range(2):
        sm = lax.dot_general(q[:, h * hd:(h + 1) * hd], km[:, h * hd:(h + 1) * hd], _NT,
                             preferred_element_type=F32)
        em = jnp.exp2(sm - jnp.max(sm, axis=-1, keepdims=True))
        ls.append(jnp.sum(em, axis=-1, keepdims=True))
        ems.append(em.astype(BF16))
    acc = jnp.dot(jnp.concatenate(ems, axis=0), vm_ref[...], preferred_element_type=F32)
    lam = _lambda_value(lam_ref, lambda_init)
    o = _diff_epilogue(acc, ls[0], ls[1], lam, g_ref[...], gate_ref[...], lambda_init)
    o_ref[...] = o.astype(o_ref.dtype)


def _attn_a_meta(qkvg_meta, lam_vecs, subln_g, *, lambda_init):
    hw = 2 * A_HEAD_DIM
    nh = A_HEADS
    kern = functools.partial(_attn_a_meta_kernel, lambda_init=lambda_init)
    return pl.pallas_call(
        kern,
        grid=(nh,),
        in_specs=[
            pl.BlockSpec((4, A_HEAD_DIM), lambda h: (0, 0)),
            pl.BlockSpec((META_ROWS, hw), lambda h: (0, h)),
            pl.BlockSpec((N_META, hw), lambda h: (0, nh + h)),
            pl.BlockSpec((N_META, hw), lambda h: (0, 2 * nh + h)),
            pl.BlockSpec((META_ROWS, hw), lambda h: (0, 3 * nh + h)),
            pl.BlockSpec((1, hw), lambda h: (0, 0)),
        ],
        out_specs=pl.BlockSpec((META_ROWS, hw), lambda h: (0, h)),
        out_shape=jax.ShapeDtypeStruct((META_ROWS, D_MODEL), BF16),
        compiler_params=_params("parallel"),
        name="diff_attn_meta",
    )(lam_vecs, qkvg_meta, qkvg_meta, qkvg_meta, qkvg_meta, subln_g.reshape(1, hw))


SWA_GROUP_COLS = 8 * B_HEAD_DIM
SWA_PAIRS = SWA_GROUP_COLS // LANES
SWA_KEYS = 256
SWA_SINK_COL = N_META


def _swa_kernel(sink_ref, q_ref, kd_ref, vd_ref, kdm_ref, vdm_ref, gate_ref, o_ref, *, n_chunks):
    step = pl.program_id(1)
    tq = CHUNK
    win = 3 * CHUNK
    nk = SWA_KEYS
    neg_inf = jnp.float32(-jnp.inf)

    colh = lax.broadcasted_iota(jnp.int32, (1, 2 * nk), 1) & (nk - 1)
    lane = lax.broadcasted_iota(jnp.int32, (nk, LANES), 1)
    lo_mask = jnp.where(lane < B_HEAD_DIM, 1.0, 0.0).astype(BF16)
    hi_mask = jnp.where(lane < B_HEAD_DIM, 0.0, 1.0).astype(BF16)
    pad = jnp.zeros((CHUNK - N_META, LANES), BF16)
    indicator = jnp.concatenate([lo_mask, hi_mask], axis=0)
    sink_rows = sink_ref[...] * LOG2E

    def block_diag(meta_rows, win_rows):
        w = jnp.concatenate([meta_rows, pad, win_rows], axis=0)
        return jnp.concatenate([w * lo_mask, w * hi_mask], axis=0)

    def chunk_body(cc, carry):
        c = step * n_chunks + cc
        r0 = pl.multiple_of(cc * CHUNK, CHUNK)
        start = pl.multiple_of(jnp.maximum(c * CHUNK - 2 * CHUNK, 0), CHUNK)
        n_vis = (jnp.minimum(c, 2) + 1) * CHUNK
        valid = jnp.logical_or(colh <= SWA_SINK_COL,
                               jnp.logical_and(colh >= CHUNK, colh < CHUNK + n_vis))
        for kv in range(B_KV_HEADS):
            cs = slice(kv * LANES, (kv + 1) * LANES)
            kbd = block_diag(kdm_ref[:, cs], kd_ref[pl.ds(start, win), cs])
            vbd = jnp.concatenate([block_diag(vdm_ref[:, cs], vd_ref[pl.ds(start, win), cs]),
                                   indicator], axis=1)
            q4 = jnp.concatenate(
                [q_ref[pl.ds(r0, tq), kv * SWA_GROUP_COLS + p * LANES: kv * SWA_GROUP_COLS + (p + 1) * LANES]
                 for p in range(SWA_PAIRS)], axis=0)
            s = lax.dot_general(q4, kbd, _NT, preferred_element_type=F32)
            p_rows = []
            for p in range(SWA_PAIRS):
                bias = jnp.where(valid, sink_rows[kv * SWA_PAIRS + p:kv * SWA_PAIRS + p + 1, :], neg_inf)
                sp = s[p * tq:(p + 1) * tq] + bias
                p_cols = []
                for hp in range(2):
                    sq = sp[:, hp * nk:(hp + 1) * nk]
                    e = jnp.exp2(sq - jnp.max(sq, axis=-1, keepdims=True))
                    p_cols.append(e.astype(BF16))
                p_rows.append(jnp.concatenate(p_cols, axis=1))
            pmat = jnp.concatenate(p_rows, axis=0)
            o4 = jnp.dot(pmat, vbd, preferred_element_type=F32)
            on = o4[:, :LANES] * (1.0 / o4[:, LANES:])
            for p in range(SWA_PAIRS):
                osl = slice(kv * SWA_GROUP_COLS + p * LANES, kv * SWA_GROUP_COLS + (p + 1) * LANES)
                gate = gate_ref[pl.ds(r0, tq), osl].astype(F32)
                o_ref[pl.ds(r0, tq), osl] = (on[p * tq:(p + 1) * tq]
                                             * (gate * jax.nn.sigmoid(gate))).astype(o_ref.dtype)
        return carry

    lax.fori_loop(0, n_chunks, chunk_body, 0, unroll=8)


def _swa(qg, kvd, kvd_meta, sinks, *, batch, seq, n_chunks):
    ns = seq // (CHUNK * n_chunks)
    rows = CHUNK * n_chunks
    kw = B_KV_HEADS * LANES
    sink_rows = jnp.zeros((B_KV_HEADS * SWA_PAIRS, 2, SWA_KEYS), F32).at[:, :, SWA_SINK_COL].set(
        sinks.astype(F32).reshape(B_KV_HEADS * SWA_PAIRS, 2)).reshape(B_KV_HEADS * SWA_PAIRS, 2 * SWA_KEYS)
    return pl.pallas_call(
        functools.partial(_swa_kernel, n_chunks=n_chunks),
        grid=(batch, ns),
        in_specs=[
            pl.BlockSpec((B_KV_HEADS * SWA_PAIRS, 2 * SWA_KEYS), lambda b, c: (0, 0)),
            pl.BlockSpec((rows, D_MODEL), lambda b, c: (b * ns + c, 0)),
            pl.BlockSpec((seq, kw), lambda b, c: (b, 0)),
            pl.BlockSpec((seq, kw), lambda b, c: (b, 1)),
            pl.BlockSpec((N_META, kw), lambda b, c: (0, 0)),
            pl.BlockSpec((N_META, kw), lambda b, c: (0, 1)),
            pl.BlockSpec((rows, D_MODEL), lambda b, c: (b * ns + c, 1)),
        ],
        out_specs=pl.BlockSpec((rows, D_MODEL), lambda b, c: (b * ns + c, 0)),
        out_shape=jax.ShapeDtypeStruct((batch * seq, D_MODEL), BF16),
        compiler_params=_params("parallel", "arbitrary"),
        name="swa_attn",
    )(sink_rows, qg, kvd, kvd, kvd_meta, kvd_meta, qg)


def _out_kernel(*refs, n_norm, emit_h):
    o_ref, w_ref, h_ref = refs[:3]
    g_refs = refs[3:3 + n_norm]
    outs = refs[3 + n_norm:]
    tm = o_ref.shape[0]
    sub = min(tm, PROJ_SUB_ROWS)
    w = w_ref[...]
    for rb in range(tm // sub):
        rows = slice(rb * sub, (rb + 1) * sub)
        acc = jnp.dot(o_ref[rows, :], w, preferred_element_type=F32) + h_ref[rows, :]
        norm_outs = outs
        if emit_h:
            outs[0][rows, :] = acc
            norm_outs = outs[1:]
        r = lax.rsqrt(jnp.mean(acc * acc, axis=-1, keepdims=True) + NORM_EPS)
        xn = acc * r
        for g_ref, n_ref in zip(g_refs, norm_outs):
            n_ref[rows, :] = (xn * g_ref[...]).astype(n_ref.dtype)


def _out_proj(o, w, layer, h, gains, *, tm, emit_h, norm_dtype, o_row_block=None):
    m = h.shape[0]
    row = pl.BlockSpec((tm, D_MODEL), lambda i: (i, 0))
    o_spec = row if o_row_block is None else pl.BlockSpec((tm, D_MODEL), lambda i: (o_row_block(i), 0))
    gspec = pl.BlockSpec((1, D_MODEL), lambda i: (0, 0))
    out_shape = ([jax.ShapeDtypeStruct((m, D_MODEL), F32)] if emit_h else []) + [
        jax.ShapeDtypeStruct((m, D_MODEL), norm_dtype) for _ in gains]
    kern = functools.partial(_out_kernel, n_norm=len(gains), emit_h=emit_h)
    return pl.pallas_call(
        kern,
        grid=(m // tm,),
        in_specs=[o_spec, pl.BlockSpec((None, D_MODEL, D_MODEL), lambda i: (layer, 0, 0)), row]
        + [gspec] * len(gains),
        out_specs=[row] * len(out_shape),
        out_shape=out_shape,
        compiler_params=_params("parallel"),
        name="out_proj",
    )(o, w, h, *[g.reshape(1, D_MODEL) for g in gains])


def kernel(x, meta_tokens, a_norm, a_w_in, a_w_out, a_lambda_q1, a_lambda_k1, a_lambda_q2,
           a_lambda_k2, a_subln, kv_norm, w_kv, b_norm, b_w_in, b_w_out, b_sinks, final_norm):
    batch, seq, d = x.shape
    n_a = a_w_in.shape[0]
    n_b = b_w_in.shape[0]
    h_r = x.reshape(batch * seq, d)
    h_m = jnp.concatenate([meta_tokens.astype(x.dtype),
                           jnp.zeros((META_ROWS - N_META, d), x.dtype)], axis=0)

    tm_r = 1024
    tm_out = 512
    tq_a = 512
    a_scale = A_HEAD_DIM ** -0.5 * LOG2E
    b_scale = B_HEAD_DIM ** -0.5 * LOG2E
    attn_rows = functools.partial(_attn_a_row_block, batch=batch, seq=seq, tq=tq_a, tm=tm_out)

    pos_r = jnp.arange(seq, dtype=jnp.int32) + N_META
    pos_m = jnp.arange(META_ROWS, dtype=jnp.int32)
    tab_a_r = _proj_tables(pos_r, A_HEAD_DIM, A_ROT_HALF, a_scale, tm_r)
    tab_a_m = _proj_tables(pos_m, A_HEAD_DIM, A_ROT_HALF, a_scale, META_ROWS)
    tab_b_r = _proj_tables(pos_r, B_HEAD_DIM, B_ROT_HALF, b_scale, tm_r)
    tab_b_m = _proj_tables(pos_m, B_HEAD_DIM, B_ROT_HALF, b_scale, META_ROWS)

    wk, wv = jnp.split(w_kv, 2, axis=1)

    def dup_heads(w):
        w = w.reshape(d, B_KV_HEADS, 1, B_HEAD_DIM)
        return jnp.broadcast_to(w, (d, B_KV_HEADS, 2, B_HEAD_DIM)).reshape(d, B_KV_HEADS * LANES)

    w_kvd = jnp.concatenate([dup_heads(wk), dup_heads(wv)], axis=1)

    a_w_out_b = a_w_out.astype(BF16)
    b_w_out_b = b_w_out.astype(BF16)
    hn_r = _rms_norm(h_r, a_norm[0], 512)
    hn_m = _rms_norm(h_m, a_norm[0], META_ROWS)
    for l in range(n_a):
        lambda_init = 0.8 - 0.6 * math.exp(-0.3 * l)
        proj_kw = dict(tn=1024, rope_cols=2 * d, q_cols=d, half=A_ROT_HALF)
        qkvg_r, qkvg_m = _proj(hn_r, a_w_in, l, tab_a_r, tm=tm_r, a_meta=hn_m, tables_meta=tab_a_m,
                                **proj_kw)
        lam_vecs = jnp.stack([a_lambda_q1[l], a_lambda_k1[l], a_lambda_q2[l], a_lambda_k2[l]]).astype(F32)
        o_r = _attn_a(qkvg_r, qkvg_m, lam_vecs, a_subln[l], batch=batch, seq=seq, tq=tq_a,
                      lambda_init=lambda_init)
        o_m = _attn_a_meta(qkvg_m, lam_vecs, a_subln[l], lambda_init=lambda_init)
        if l + 1 < n_a:
            gains_r = gains_m = [a_norm[l + 1]]
        else:
            gains_r, gains_m = [kv_norm, b_norm[0]], [kv_norm]
        outs_r = _out_proj(o_r.reshape(batch * seq, d), a_w_out_b, l, h_r, gains_r, tm=tm_out,
                           emit_h=True, norm_dtype=BF16, o_row_block=attn_rows)
        outs_m = _out_proj(o_m, a_w_out_b, l, h_m, gains_m, tm=META_ROWS, emit_h=True,
                           norm_dtype=BF16)
        h_r, h_m = outs_r[0], outs_m[0]
        hn_r, hn_m = outs_r[1], outs_m[1]

    kv_kw = dict(tn=512, rope_cols=B_KV_HEADS * LANES, q_cols=0, half=B_ROT_HALF)
    kvd_r, kvd_m = _proj(hn_r, w_kvd[None], 0, tab_b_r, tm=tm_r, a_meta=hn_m, tables_meta=tab_b_m,
                         **kv_kw)
    hn_r = outs_r[2]

    for l in range(n_b):
        qg = _proj(hn_r, b_w_in, l, tab_b_r, tm=tm_r, tn=1024, rope_cols=d, q_cols=d,
                   half=B_ROT_HALF)
        o_r = _swa(qg, kvd_r, kvd_m, b_sinks[l], batch=batch, seq=seq, n_chunks=8)
        if l + 1 < n_b:
            h_r, hn_r = _out_proj(o_r, b_w_out_b, l, h_r, [b_norm[l + 1]], tm=tm_out, emit_h=True,
                                  norm_dtype=BF16)
        else:
            (out,) = _out_proj(o_r, b_w_out_b, l, h_r, [final_norm], tm=tm_out, emit_h=False,
                               norm_dtype=x.dtype)
    return out.reshape(batch, seq, d)
```

```python
import functools
import math

import jax
import jax.numpy as jnp
from jax import lax
from jax.experimental import pallas as pl
from jax.experimental.pallas import tpu as pltpu

D_MODEL = 2048
N_META = 16
META_ROWS = 64
CHUNK = 64
CHUNK_SHIFT = 6
ROPE_THETA = 500000.0
NORM_EPS = 1e-5
A_HEADS = 8
A_HEAD_DIM = 128
A_ROT_HALF = 16
B_HEAD_DIM = 64
B_KV_HEADS = 4
B_ROT_HALF = 8
LANES = 128
VMEM_LIMIT = 56 * 1024 * 1024
LOG2E = 1.4426950408889634

F32 = jnp.float32
BF16 = jnp.bfloat16
_NT = (((1,), (1,)), ((), ()))


def _params(*sem):
    return pltpu.CompilerParams(dimension_semantics=sem, vmem_limit_bytes=VMEM_LIMIT)


def _norm_kernel(x_ref, g_ref, o_ref):
    x = x_ref[...]
    r = lax.rsqrt(jnp.mean(x * x, axis=-1, keepdims=True) + NORM_EPS)
    o_ref[...] = (x * r * g_ref[...]).astype(o_ref.dtype)


def _rms_norm(x, g, tm):
    m = x.shape[0]
    return pl.pallas_call(
        _norm_kernel,
        grid=(m // tm,),
        in_specs=[pl.BlockSpec((tm, D_MODEL), lambda i: (i, 0)),
                  pl.BlockSpec((1, D_MODEL), lambda i: (0, 0))],
        out_specs=pl.BlockSpec((tm, D_MODEL), lambda i: (i, 0)),
        out_shape=jax.ShapeDtypeStruct((m, D_MODEL), BF16),
        compiler_params=_params("parallel"),
        name="rms_norm",
    )(x, g.reshape(1, D_MODEL))


PROJ_SUB_ROWS = 128
DIAG_PIECES = 4


def _proj_rows(a_ref, w, c_ref, s1_ref, s2_ref, o_ref, half):
    tm, tn = o_ref.shape
    sub = min(tm, PROJ_SUB_ROWS)
    for rb in range(tm // sub):
        rows = slice(rb * sub, (rb + 1) * sub)
        acc = jnp.dot(a_ref[rows, :], w, preferred_element_type=F32)
        c, s1, s2 = c_ref[rows, :], s1_ref[rows, :], s2_ref[rows, :]
        for cb in range(tn // LANES):
            x = acc[:, cb * LANES:(cb + 1) * LANES]
            y = (x * c + pltpu.roll(x, LANES - half, 1) * s1 + pltpu.roll(x, half, 1) * s2)
            o_ref[rows, cb * LANES:(cb + 1) * LANES] = y.astype(o_ref.dtype)


def _proj_kernel(*refs, half, with_meta):
    if with_meta:
        (a_ref, w_ref, c_ref, s1_ref, s2_ref, am_ref, cm_ref, s1m_ref, s2m_ref,
         o_ref, om_ref, wb_ref) = refs
    else:
        a_ref, w_ref, c_ref, s1_ref, s2_ref, o_ref, wb_ref = refs

    @pl.when(pl.program_id(1) == 0)
    def _():
        wb_ref[...] = w_ref[...].astype(wb_ref.dtype)
        if with_meta:
            _proj_rows(am_ref, wb_ref[...], cm_ref, s1m_ref, s2m_ref, om_ref, half)

    _proj_rows(a_ref, wb_ref[...], c_ref, s1_ref, s2_ref, o_ref, half)


def _proj(a, w, layer, tables, *, tm, tn, rope_cols, q_cols, half, a_meta=None, tables_meta=None):
    m, n = a.shape[0], w.shape[2]
    tab_blocks = (tables[0].shape[0] - tm) // (2 * tm)
    n_q_tiles, n_rope_tiles = q_cols // tn, rope_cols // tn
    with_meta = a_meta is not None

    def kind(j):
        return jnp.where(j < n_q_tiles, 0, jnp.where(j < n_rope_tiles, 1, 2))

    n_seq = (m // tm) // tab_blocks

    def row_tile(i):
        return (i % n_seq) * tab_blocks + i // n_seq

    tab_spec = pl.BlockSpec((tm, LANES), lambda j, i: (kind(j) * tab_blocks + jnp.where(
        kind(j) == 2, 0, i // n_seq), 0))
    in_specs = [pl.BlockSpec((tm, D_MODEL), lambda j, i: (row_tile(i), 0)),
                pl.BlockSpec((None, D_MODEL, tn), lambda j, i: (layer, 0, j)),
                tab_spec, tab_spec, tab_spec]
    out_specs = [pl.BlockSpec((tm, tn), lambda j, i: (row_tile(i), j))]
    out_shape = [jax.ShapeDtypeStruct((m, n), BF16)]
    operands = [a, w, *tables]
    if with_meta:
        mm = a_meta.shape[0]
        tabm_spec = pl.BlockSpec((mm, LANES), lambda j, i: (kind(j), 0))
        in_specs += [pl.BlockSpec((mm, D_MODEL), lambda j, i: (0, 0)), tabm_spec, tabm_spec, tabm_spec]
        out_specs.append(pl.BlockSpec((mm, tn), lambda j, i: (0, j)))
        out_shape.append(jax.ShapeDtypeStruct((mm, n), BF16))
        operands += [a_meta, *tables_meta]
    outs = pl.pallas_call(
        functools.partial(_proj_kernel, half=half, with_meta=with_meta),
        grid=(n // tn, m // tm),
        in_specs=in_specs,
        out_specs=out_specs,
        out_shape=out_shape,
        scratch_shapes=[pltpu.VMEM((D_MODEL, tn), BF16)],
        compiler_params=_params("arbitrary", "arbitrary"),
        name="proj_rope",
    )(*operands)
    return outs if with_meta else outs[0]


def _proj_tables(pos, head_dim, half, q_scale, tm):
    ident = (jnp.ones((tm, LANES), F32), jnp.zeros((tm, LANES), F32), jnp.zeros((tm, LANES), F32))
    return tuple(jnp.concatenate([t * q_scale, t, e], axis=0)
                 for t, e in zip(_rope_tables(pos, head_dim, half), ident))


def _rope_tables(pos, head_dim, half):
    inv = ROPE_THETA ** (-jnp.arange(0, 2 * half, 2, dtype=F32) / (2 * half))
    ang = pos.astype(F32)[:, None] * inv[None, :]
    cos, sin = jnp.cos(ang), jnp.sin(ang)
    p = pos.shape[0]
    ones = jnp.ones((p, head_dim - 2 * half), F32)
    zeros = jnp.zeros((p, head_dim - 2 * half), F32)
    zh = jnp.zeros((p, half), F32)
    c = jnp.concatenate([cos, cos, ones], axis=1)
    s1 = jnp.concatenate([-sin, zh, zeros], axis=1)
    s2 = jnp.concatenate([zh, sin, zeros], axis=1)
    rep = LANES // head_dim
    return tuple(jnp.tile(t, (1, rep)) for t in (c, s1, s2))


def _lambda_value(lam_ref, lambda_init):
    lv = lam_ref[...]
    return (jnp.exp(jnp.sum(lv[0:1] * lv[1:2], axis=-1, keepdims=True))
            - jnp.exp(jnp.sum(lv[2:3] * lv[3:4], axis=-1, keepdims=True)) + lambda_init)


def _diff_epilogue(acc, l1, l2, lam, g, gate, lambda_init):
    tq = acc.shape[0] // 2
    o = acc[:tq] * (1.0 / l1) - acc[tq:] * (lam / l2)
    r = lax.rsqrt(jnp.mean(o * o, axis=-1, keepdims=True) + NORM_EPS)
    o = o * r * (g * (1.0 - lambda_init))
    gate = gate.astype(F32)
    return o * (gate * jax.nn.sigmoid(gate))


def _lane_block_reduce(x, op):
    r = x[:, :LANES]
    for c in range(1, x.shape[1] // LANES):
        r = op(r, x[:, c * LANES:(c + 1) * LANES])
    return r


def _attn_a_kernel(lam_ref, dmat_ref, qa_ref, qb_ref, k_ref, v_ref, km_ref, vm_ref, ga_ref, gb_ref,
                   g_ref, o_ref, q_sc, s_ref, mr_ref, mb_ref, ls_ref, acc_ref, *, tq, n_tiles,
                   lambda_init):
    i = pl.program_id(2)
    hd = A_HEAD_DIM
    n_steps = n_tiles + 1
    hq = tq // DIAG_PIECES
    neg_inf = jnp.float32(-jnp.inf)
    q_sc[0] = qa_ref[...]
    q_sc[1] = qb_ref[...]
    km = km_ref[...]
    vm = vm_ref[...]
    lam = _lambda_value(lam_ref, lambda_init)
    vis = dmat_ref[...] <= 0

    def step_info(t):
        if t == 0:
            return 0, i, True
        if t == n_steps - 1:
            return 1, n_tiles - 1 - i, True
        if t > n_tiles // 2 - 1:
            return 1, t - i - 1, False
        low = t <= i
        return jnp.where(low, 0, 1), jnp.where(low, i - t, t - i - 1), False

    def parts(diag):
        if not diag:
            return ((slice(0, tq), tq),)
        return tuple((slice(r * hq, (r + 1) * hq), (r + 1) * hq) for r in range(DIAG_PIECES))

    init = jnp.full((tq, LANES), neg_inf, F32)
    for h in range(2):
        for sel in range(2):
            mr_ref[h, sel] = init
            ls_ref[h, sel] = jnp.zeros((tq, LANES), F32)
    acc_ref[...] = jnp.zeros(acc_ref.shape, F32)
    for t in range(n_steps):
        sel, kt, diag = step_info(t)
        krows = k_ref[pl.ds(pl.multiple_of(kt * tq, tq), tq), :]
        for h in range(2):
            for rows, nk in parts(diag):
                s = lax.dot_general(q_sc[sel, rows, h * hd:(h + 1) * hd],
                                    krows[:nk, h * hd:(h + 1) * hd], _NT,
                                    preferred_element_type=F32)
                if diag:
                    masked = jnp.where(vis, s[:, nk - hq:], neg_inf)
                    s = masked if nk == hq else jnp.concatenate([s[:, :nk - hq], masked], axis=1)
                s_ref[h, t, rows, :nk] = s
                mr_ref[h, sel, rows, :] = jnp.maximum(mr_ref[h, sel, rows, :],
                                                      _lane_block_reduce(s, jnp.maximum))

    lane = lax.broadcasted_iota(jnp.int32, km.shape, 1)
    head0 = jnp.where(lane < hd, 1.0, 0.0).astype(BF16)
    head1 = jnp.where(lane < hd, 0.0, 1.0).astype(BF16)
    km_bd = jnp.concatenate([km * head0, km * head1], axis=0)
    zero_v = jnp.zeros_like(vm)
    vm_bd = jnp.concatenate([jnp.concatenate([vm, zero_v], axis=1),
                             jnp.concatenate([zero_v, vm], axis=1)], axis=0)
    first = lax.broadcasted_iota(jnp.int32, (1, 2 * N_META), 1) < N_META
    sms = [lax.dot_general(q_sc[sel], km_bd, _NT, preferred_element_type=F32) for sel in range(2)]

    for sel in range(2):
        for h in range(2):
            own = first if h == 0 else jnp.logical_not(first)
            m = jnp.maximum(jnp.max(jnp.where(own, sms[sel], neg_inf), axis=-1, keepdims=True),
                            jnp.max(mr_ref[h, sel], axis=-1, keepdims=True))
            mb_ref[h, sel] = jnp.broadcast_to(m, (tq, LANES))

    for t in range(n_steps):
        sel, kt, diag = step_info(t)
        vrows = v_ref[pl.ds(pl.multiple_of(kt * tq, tq), tq), :]
        for rows, nk in parts(diag):
            ps = []
            for h in range(2):
                mb = mb_ref[h, sel, rows, :]
                blocks = []
                ls = None
                for c in range(nk // LANES):
                    e = jnp.exp2(s_ref[h, t, rows, c * LANES:(c + 1) * LANES] - mb)
                    ls = e if ls is None else ls + e
                    blocks.append(e.astype(BF16))
                ls_ref[h, sel, rows, :] = ls_ref[h, sel, rows, :] + ls
                ps.append(jnp.concatenate(blocks, axis=1))
            pv = jnp.dot(jnp.concatenate(ps, axis=0), vrows[:nk], preferred_element_type=F32)
            n = rows.stop - rows.start
            for h in range(2):
                arows = slice(h * tq + rows.start, h * tq + rows.stop)
                acc_ref[sel, arows, :] = acc_ref[sel, arows, :] + pv[h * n:(h + 1) * n]

    for sel, gate_ref in ((0, ga_ref), (1, gb_ref)):
        em = jnp.exp2(sms[sel] - jnp.where(first, mb_ref[0, sel, :, :1], mb_ref[1, sel, :, :1]))
        ls = [jnp.sum(ls_ref[0, sel], axis=-1, keepdims=True)
              + jnp.sum(jnp.where(first, em, 0.0), axis=-1, keepdims=True),
              jnp.sum(ls_ref[1, sel], axis=-1, keepdims=True)
              + jnp.sum(jnp.where(first, 0.0, em), axis=-1, keepdims=True)]
        pv = jnp.dot(em.astype(BF16), vm_bd, preferred_element_type=F32)
        acc = acc_ref[sel] + jnp.concatenate([pv[:, :2 * hd], pv[:, 2 * hd:]], axis=0)
        o = _diff_epilogue(acc, ls[0], ls[1], lam, g_ref[...], gate_ref[...], lambda_init)
        o_ref[sel] = o.astype(o_ref.dtype)


def _attn_a(qkvg, qkvg_meta, lam_vecs, subln_g, *, batch, seq, tq, lambda_init):
    n_tiles = seq // tq
    half = n_tiles // 2
    hw = 2 * A_HEAD_DIM
    nh = A_HEADS
    piece = tq // DIAG_PIECES
    idx = jnp.arange(piece, dtype=jnp.int32) >> CHUNK_SHIFT
    dmat = idx[None, :] - idx[:, None]
    kern = functools.partial(_attn_a_kernel, tq=tq, n_tiles=n_tiles, lambda_init=lambda_init)
    lo = lambda b, h, i: b * n_tiles + i
    hi = lambda b, h, i: b * n_tiles + n_tiles - 1 - i
    return pl.pallas_call(
        kern,
        grid=(batch, nh, half),
        in_specs=[
            pl.BlockSpec((4, A_HEAD_DIM), lambda b, h, i: (0, 0)),
            pl.BlockSpec((piece, piece), lambda b, h, i: (0, 0)),
            pl.BlockSpec((tq, hw), lambda b, h, i: (lo(b, h, i), h)),
            pl.BlockSpec((tq, hw), lambda b, h, i: (hi(b, h, i), h)),
            pl.BlockSpec((seq, hw), lambda b, h, i: (b, nh + h)),
            pl.BlockSpec((seq, hw), lambda b, h, i: (b, 2 * nh + h)),
            pl.BlockSpec((N_META, hw), lambda b, h, i: (0, nh + h)),
            pl.BlockSpec((N_META, hw), lambda b, h, i: (0, 2 * nh + h)),
            pl.BlockSpec((tq, hw), lambda b, h, i: (lo(b, h, i), 3 * nh + h)),
            pl.BlockSpec((tq, hw), lambda b, h, i: (hi(b, h, i), 3 * nh + h)),
            pl.BlockSpec((1, hw), lambda b, h, i: (0, 0)),
        ],
        out_specs=pl.BlockSpec((2, None, None, tq, hw), lambda b, h, i: (0, b, i, 0, h)),
        out_shape=jax.ShapeDtypeStruct((2, batch, half, tq, D_MODEL), BF16),
        scratch_shapes=[pltpu.VMEM((2, tq, hw), BF16),
                        pltpu.VMEM((2, n_tiles + 1, tq, tq), F32),
                        pltpu.VMEM((2, 2, tq, LANES), F32),
                        pltpu.VMEM((2, 2, tq, LANES), F32),
                        pltpu.VMEM((2, 2, tq, LANES), F32),
                        pltpu.VMEM((2, 2 * tq, hw), F32)],
        compiler_params=_params("parallel", "parallel", "arbitrary"),
        name="diff_attn",
    )(lam_vecs, dmat, qkvg, qkvg, qkvg, qkvg, qkvg_meta, qkvg_meta, qkvg, qkvg,
      subln_g.reshape(1, hw))


def _attn_a_row_block(r, *, batch, seq, tq, tm):
    n_tiles = seq // tq
    half = n_tiles // 2
    sub = tq // tm
    b = r // (n_tiles * sub)
    u = r % (n_tiles * sub)
    tile, within = u // sub, u % sub
    hi = tile // half
    pos = jnp.where(hi == 0, tile, n_tiles - 1 - tile)
    return ((hi * batch + b) * half + pos) * sub + within


def _attn_a_meta_kernel(lam_ref, q_ref, km_ref, vm_ref, gate_ref, g_ref, o_ref, *, lambda_init):
    hd = A_HEAD_DIM
    q = q_ref[...]
    km = km_ref[...]
    ems, ls = [], []
    for h in range(2):
        sm = lax.dot_general(q[:, h * hd:(h + 1) * hd], km[:, h * hd:(h + 1) * hd], _NT,
                             preferred_element_type=F32)
        em = jnp.exp2(sm - jnp.max(sm, axis=-1, keepdims=True))
        ls.append(jnp.sum(em, axis=-1, keepdims=True))
        ems.append(em.astype(BF16))
    acc = jnp.dot(jnp.concatenate(ems, axis=0), vm_ref[...], preferred_element_type=F32)
    lam = _lambda_value(lam_ref, lambda_init)
    o = _diff_epilogue(acc, ls[0], ls[1], lam, g_ref[...], gate_ref[...], lambda_init)
    o_ref[...] = o.astype(o_ref.dtype)


def _attn_a_meta(qkvg_meta, lam_vecs, subln_g, *, lambda_init):
    hw = 2 * A_HEAD_DIM
    nh = A_HEADS
    kern = functools.partial(_attn_a_meta_kernel, lambda_init=lambda_init)
    return pl.pallas_call(
        kern,
        grid=(nh,),
        in_specs=[
            pl.BlockSpec((4, A_HEAD_DIM), lambda h: (0, 0)),
            pl.BlockSpec((META_ROWS, hw), lambda h: (0, h)),
            pl.BlockSpec((N_META, hw), lambda h: (0, nh + h)),
            pl.BlockSpec((N_META, hw), lambda h: (0, 2 * nh + h)),
            pl.BlockSpec((META_ROWS, hw), lambda h: (0, 3 * nh + h)),
            pl.BlockSpec((1, hw), lambda h: (0, 0)),
        ],
        out_specs=pl.BlockSpec((META_ROWS, hw), lambda h: (0, h)),
        out_shape=jax.ShapeDtypeStruct((META_ROWS, D_MODEL), BF16),
        compiler_params=_params("parallel"),
        name="diff_attn_meta",
    )(lam_vecs, qkvg_meta, qkvg_meta, qkvg_meta, qkvg_meta, subln_g.reshape(1, hw))


SWA_GROUP_COLS = 8 * B_HEAD_DIM
SWA_PAIRS = SWA_GROUP_COLS // LANES
SWA_KEYS = 256
SWA_SINK_COL = N_META


def _swa_kernel(sink_ref, q_ref, kd_ref, vd_ref, kdm_ref, vdm_ref, gate_ref, o_ref, *, n_chunks):
    step = pl.program_id(1)
    tq = CHUNK
    win = 3 * CHUNK
    nk = SWA_KEYS
    neg_inf = jnp.float32(-jnp.inf)

    colh = lax.broadcasted_iota(jnp.int32, (1, 2 * nk), 1) & (nk - 1)
    lane = lax.broadcasted_iota(jnp.int32, (nk, LANES), 1)
    lo_mask = jnp.where(lane < B_HEAD_DIM, 1.0, 0.0).astype(BF16)
    hi_mask = jnp.where(lane < B_HEAD_DIM, 0.0, 1.0).astype(BF16)
    pad = jnp.zeros((CHUNK - N_META, LANES), BF16)
    indicator = jnp.concatenate([lo_mask, hi_mask], axis=0)
    sink_rows = sink_ref[...] * LOG2E

    def block_diag(meta_rows, win_rows):
        w = jnp.concatenate([meta_rows, pad, win_rows], axis=0)
        return jnp.concatenate([w * lo_mask, w * hi_mask], axis=0)

    def chunk_body(cc, carry):
        c = step * n_chunks + cc
        r0 = pl.multiple_of(cc * CHUNK, CHUNK)
        start = pl.multiple_of(jnp.maximum(c * CHUNK - 2 * CHUNK, 0), CHUNK)
        n_vis = (jnp.minimum(c, 2) + 1) * CHUNK
        valid = jnp.logical_or(colh <= SWA_SINK_COL,
                               jnp.logical_and(colh >= CHUNK, colh < CHUNK + n_vis))
        for kv in range(B_KV_HEADS):
            cs = slice(kv * LANES, (kv + 1) * LANES)
            kbd = block_diag(kdm_ref[:, cs], kd_ref[pl.ds(start, win), cs])
            vbd = jnp.concatenate([block_diag(vdm_ref[:, cs], vd_ref[pl.ds(start, win), cs]),
                                   indicator], axis=1)
            q4 = jnp.concatenate(
                [q_ref[pl.ds(r0, tq), kv * SWA_GROUP_COLS + p * LANES: kv * SWA_GROUP_COLS + (p + 1) * LANES]
                 for p in range(SWA_PAIRS)], axis=0)
            s = lax.dot_general(q4, kbd, _NT, preferred_element_type=F32)
            p_rows = []
            for p in range(SWA_PAIRS):
                bias = jnp.where(valid, sink_rows[kv * SWA_PAIRS + p:kv * SWA_PAIRS + p + 1, :], neg_inf)
                sp = s[p * tq:(p + 1) * tq] + bias
                p_cols = []
                for hp in range(2):
                    sq = sp[:, hp * nk:(hp + 1) * nk]
                    e = jnp.exp2(sq - jnp.max(sq, axis=-1, keepdims=True))
                    p_cols.append(e.astype(BF16))
                p_rows.append(jnp.concatenate(p_cols, axis=1))
            pmat = jnp.concatenate(p_rows, axis=0)
            o4 = jnp.dot(pmat, vbd, preferred_element_type=F32)
            on = o4[:, :LANES] * (1.0 / o4[:, LANES:])
            for p in range(SWA_PAIRS):
                osl = slice(kv * SWA_GROUP_COLS + p * LANES, kv * SWA_GROUP_COLS + (p + 1) * LANES)
                gate = gate_ref[pl.ds(r0, tq), osl].astype(F32)
                o_ref[pl.ds(r0, tq), osl] = (on[p * tq:(p + 1) * tq]
                                             * (gate * jax.nn.sigmoid(gate))).astype(o_ref.dtype)
        return carry

    lax.fori_loop(0, n_chunks, chunk_body, 0, unroll=8)


def _swa(qg, kvd, kvd_meta, sinks, *, batch, seq, n_chunks):
    ns = seq // (CHUNK * n_chunks)
    rows = CHUNK * n_chunks
    kw = B_KV_HEADS * LANES
    sink_rows = jnp.zeros((B_KV_HEADS * SWA_PAIRS, 2, SWA_KEYS), F32).at[:, :, SWA_SINK_COL].set(
        sinks.astype(F32).reshape(B_KV_HEADS * SWA_PAIRS, 2)).reshape(B_KV_HEADS * SWA_PAIRS, 2 * SWA_KEYS)
    return pl.pallas_call(
        functools.partial(_swa_kernel, n_chunks=n_chunks),
        grid=(batch, ns),
        in_specs=[
            pl.BlockSpec((B_KV_HEADS * SWA_PAIRS, 2 * SWA_KEYS), lambda b, c: (0, 0)),
            pl.BlockSpec((rows, D_MODEL), lambda b, c: (b * ns + c, 0)),
            pl.BlockSpec((seq, kw), lambda b, c: (b, 0)),
            pl.BlockSpec((seq, kw), lambda b, c: (b, 1)),
            pl.BlockSpec((N_META, kw), lambda b, c: (0, 0)),
            pl.BlockSpec((N_META, kw), lambda b, c: (0, 1)),
            pl.BlockSpec((rows, D_MODEL), lambda b, c: (b * ns + c, 1)),
        ],
        out_specs=pl.BlockSpec((rows, D_MODEL), lambda b, c: (b * ns + c, 0)),
        out_shape=jax.ShapeDtypeStruct((batch * seq, D_MODEL), BF16),
        compiler_params=_params("parallel", "arbitrary"),
        name="swa_attn",
    )(sink_rows, qg, kvd, kvd, kvd_meta, kvd_meta, qg)


def _out_kernel(*refs, n_norm, emit_h):
    o_ref, w_ref, h_ref = refs[:3]
    g_refs = refs[3:3 + n_norm]
    outs = refs[3 + n_norm:]
    tm = o_ref.shape[0]
    sub = min(tm, PROJ_SUB_ROWS)
    w = w_ref[...]
    for rb in range(tm // sub):
        rows = slice(rb * sub, (rb + 1) * sub)
        acc = jnp.dot(o_ref[rows, :], w, preferred_element_type=F32) + h_ref[rows, :]
        norm_outs = outs
        if emit_h:
            outs[0][rows, :] = acc
            norm_outs = outs[1:]
        r = lax.rsqrt(jnp.mean(acc * acc, axis=-1, keepdims=True) + NORM_EPS)
        xn = acc * r
        for g_ref, n_ref in zip(g_refs, norm_outs):
            n_ref[rows, :] = (xn * g_ref[...]).astype(n_ref.dtype)


def _out_proj(o, w, layer, h, gains, *, tm, emit_h, norm_dtype, o_row_block=None):
    m = h.shape[0]
    row = pl.BlockSpec((tm, D_MODEL), lambda i: (i, 0))
    o_spec = row if o_row_block is None else pl.BlockSpec((tm, D_MODEL), lambda i: (o_row_block(i), 0))
    gspec = pl.BlockSpec((1, D_MODEL), lambda i: (0, 0))
    out_shape = ([jax.ShapeDtypeStruct((m, D_MODEL), F32)] if emit_h else []) + [
        jax.ShapeDtypeStruct((m, D_MODEL), norm_dtype) for _ in gains]
    kern = functools.partial(_out_kernel, n_norm=len(gains), emit_h=emit_h)
    return pl.pallas_call(
        kern,
        grid=(m // tm,),
        in_specs=[o_spec, pl.BlockSpec((None, D_MODEL, D_MODEL), lambda i: (layer, 0, 0)), row]
        + [gspec] * len(gains),
        out_specs=[row] * len(out_shape),
        out_shape=out_shape,
        compiler_params=_params("parallel"),
        name="out_proj",
    )(o, w, h, *[g.reshape(1, D_MODEL) for g in gains])


def kernel(x, meta_tokens, a_norm, a_w_in, a_w_out, a_lambda_q1, a_lambda_k1, a_lambda_q2,
           a_lambda_k2, a_subln, kv_norm, w_kv, b_norm, b_w_in, b_w_out, b_sinks, final_norm):
    batch, seq, d = x.shape
    n_a = a_w_in.shape[0]
    n_b = b_w_in.shape[0]
    h_r = x.reshape(batch * seq, d)
    h_m = jnp.concatenate([meta_tokens.astype(x.dtype),
                           jnp.zeros((META_ROWS - N_META, d), x.dtype)], axis=0)

    tm_r = 1024
    tm_out = 512
    tq_a = 512
    a_scale = A_HEAD_DIM ** -0.5 * LOG2E
    b_scale = B_HEAD_DIM ** -0.5 * LOG2E
    attn_rows = functools.partial(_attn_a_row_block, batch=batch, seq=seq, tq=tq_a, tm=tm_out)

    pos_r = jnp.arange(seq, dtype=jnp.int32) + N_META
    pos_m = jnp.arange(META_ROWS, dtype=jnp.int32)
    tab_a_r = _proj_tables(pos_r, A_HEAD_DIM, A_ROT_HALF, a_scale, tm_r)
    tab_a_m = _proj_tables(pos_m, A_HEAD_DIM, A_ROT_HALF, a_scale, META_ROWS)
    tab_b_r = _proj_tables(pos_r, B_HEAD_DIM, B_ROT_HALF, b_scale, tm_r)
    tab_b_m = _proj_tables(pos_m, B_HEAD_DIM, B_ROT_HALF, b_scale, META_ROWS)

    wk, wv = jnp.split(w_kv, 2, axis=1)

    def dup_heads(w):
        w = w.reshape(d, B_KV_HEADS, 1, B_HEAD_DIM)
        return jnp.broadcast_to(w, (d, B_KV_HEADS, 2, B_HEAD_DIM)).reshape(d, B_KV_HEADS * LANES)

    w_kvd = jnp.concatenate([dup_heads(wk), dup_heads(wv)], axis=1)

    a_w_out_b = a_w_out.astype(BF16)
    b_w_out_b = b_w_out.astype(BF16)
    hn_r = _rms_norm(h_r, a_norm[0], 512)
    hn_m = _rms_norm(h_m, a_norm[0], META_ROWS)
    for l in range(n_a):
        lambda_init = 0.8 - 0.6 * math.exp(-0.3 * l)
        proj_kw = dict(tn=1024, rope_cols=2 * d, q_cols=d, half=A_ROT_HALF)
        qkvg_r, qkvg_m = _proj(hn_r, a_w_in, l, tab_a_r, tm=tm_r, a_meta=hn_m, tables_meta=tab_a_m,
                                **proj_kw)
        lam_vecs = jnp.stack([a_lambda_q1[l], a_lambda_k1[l], a_lambda_q2[l], a_lambda_k2[l]]).astype(F32)
        o_r = _attn_a(qkvg_r, qkvg_m, lam_vecs, a_subln[l], batch=batch, seq=seq, tq=tq_a,
                      lambda_init=lambda_init)
        o_m = _attn_a_meta(qkvg_m, lam_vecs, a_subln[l], lambda_init=lambda_init)
        if l + 1 < n_a:
            gains_r = gains_m = [a_norm[l + 1]]
        else:
            gains_r, gains_m = [kv_norm, b_norm[0]], [kv_norm]
        outs_r = _out_proj(o_r.reshape(batch * seq, d), a_w_out_b, l, h_r, gains_r, tm=tm_out,
                           emit_h=True, norm_dtype=BF16, o_row_block=attn_rows)
        outs_m = _out_proj(o_m, a_w_out_b, l, h_m, gains_m, tm=META_ROWS, emit_h=True,
                           norm_dtype=BF16)
        h_r, h_m = outs_r[0], outs_m[0]
        hn_r, hn_m = outs_r[1], outs_m[1]

    kv_kw = dict(tn=512, rope_cols=B_KV_HEADS * LANES, q_cols=0, half=B_ROT_HALF)
    kvd_r, kvd_m = _proj(hn_r, w_kvd[None], 0, tab_b_r, tm=tm_r, a_meta=hn_m, tables_meta=tab_b_m,
                         **kv_kw)
    hn_r = outs_r[2]

    for l in range(n_b):
        qg = _proj(hn_r, b_w_in, l, tab_b_r, tm=tm_r, tn=1024, rope_cols=d, q_cols=d,
                   half=B_ROT_HALF)
        o_r = _swa(qg, kvd_r, kvd_m, b_sinks[l], batch=batch, seq=seq, n_chunks=16)
        if l + 1 < n_b:
            h_r, hn_r = _out_proj(o_r, b_w_out_b, l, h_r, [b_norm[l + 1]], tm=tm_out, emit_h=True,
                                  norm_dtype=BF16)
        else:
            (out,) = _out_proj(o_r, b_w_out_b, l, h_r, [final_norm], tm=tm_out, emit_h=False,
                               norm_dtype=x.dtype)
    return out.reshape(batch, seq, d)
```
